```python
import math
import jax, jax.numpy as jnp
from jax import lax
import numpy as np

D_MODEL = 2048
BATCH = 8
SEQ = 2048
DEPTH = 2
DEC_BATCH = 32
DEC_SEQ = 1
PAST_LEN = 8192
PAGE_SIZE = 128

N_ATTN = (DEPTH + 1) // 2
N_DELTA = DEPTH // 2

DA_HEAD_DIM = 64
DA_V_DIM = 2 * DA_HEAD_DIM
DA_HEADS = D_MODEL // DA_V_DIM
DA_SUB_HEADS = 2 * DA_HEADS
DA_QK_WIDTH = DA_SUB_HEADS * DA_HEAD_DIM
DA_QKV_WIDTH = 2 * DA_QK_WIDTH + DA_HEADS * DA_V_DIM
ROPE_THETA = 500000.0
ROT_DIM = DA_HEAD_DIM // 4
Q_BLOCK = 128
LAMBDA_INIT_BASE = 0.8
LAMBDA_INIT_AMP = 0.6
LAMBDA_INIT_RATE = 0.3

GD_K_HEADS = 16
GD_V_HEADS = 32
GD_K_DIM = 128
GD_V_DIM = 128
GD_KEY_WIDTH = GD_K_HEADS * GD_K_DIM
GD_VAL_WIDTH = GD_V_HEADS * GD_V_DIM
GD_CONV_CH = 2 * GD_KEY_WIDTH + GD_VAL_WIDTH
GD_IN_WIDTH = GD_CONV_CH + GD_VAL_WIDTH + 2 * GD_V_HEADS
CONV_WIDTH = 4
CHUNK = 64

D_FF = 5632
N_EXPERTS = 8
TOP_K = 2
D_FF_EXPERT = 2816

NORM_EPS = 1e-6
L2_EPS = 1e-6

kernel_name = 'hybrid_diffattn_gdn_moe_step'


def rms_norm(x, w):
    xf = x.astype(jnp.float32)
    y = xf * lax.rsqrt(jnp.mean(xf * xf, axis=-1, keepdims=True) + NORM_EPS)
    return (y * w.astype(jnp.float32)).astype(x.dtype)


def l2norm(x):
    xf = x.astype(jnp.float32)
    return xf * lax.rsqrt(jnp.sum(xf * xf, axis=-1, keepdims=True) + L2_EPS)


def partial_rope(x, pos):
    half = ROT_DIM // 2
    inv = ROPE_THETA ** (-2.0 * jnp.arange(half, dtype=jnp.float32) / ROT_DIM)
    ang = pos[:, None] * inv[None, :]
    cos = jnp.cos(ang)[None, :, None, :]
    sin = jnp.sin(ang)[None, :, None, :]
    xf = x.astype(jnp.float32)
    x1 = xf[..., :half]
    x2 = xf[..., half:ROT_DIM]
    out = jnp.concatenate([x1 * cos - x2 * sin, x2 * cos + x1 * sin, xf[..., ROT_DIM:]], axis=-1)
    return out.astype(x.dtype)


def diff_lambda(lq1, lk1, lq2, lk2, lam_init):
    f32 = jnp.float32
    return (jnp.exp(jnp.sum(lq1.astype(f32) * lk1.astype(f32)))
            - jnp.exp(jnp.sum(lq2.astype(f32) * lk2.astype(f32))) + lam_init)


def diff_attn_qkv(h, w_qkv, pos):
    b, t, _ = h.shape
    qkv = h @ w_qkv
    q = qkv[..., :DA_QK_WIDTH].reshape(b, t, DA_SUB_HEADS, DA_HEAD_DIM)
    k = qkv[..., DA_QK_WIDTH:2 * DA_QK_WIDTH].reshape(b, t, DA_SUB_HEADS, DA_HEAD_DIM)
    v = qkv[..., 2 * DA_QK_WIDTH:].reshape(b, t, DA_HEADS, DA_V_DIM)
    return partial_rope(q, pos), partial_rope(k, pos), v


def diff_combine(p, lam):
    bsz, _, nq, nk = p.shape
    p = p.reshape(bsz, DA_HEADS, 2, nq, nk)
    return p[:, :, 0] - lam * p[:, :, 1]


def diff_attention_prompt(q, k, v, lam):
    b, s = q.shape[:2]
    nb = s // Q_BLOCK
    scale = DA_HEAD_DIM ** -0.5
    qb = q.reshape(b, nb, Q_BLOCK, DA_SUB_HEADS, DA_HEAD_DIM).swapaxes(0, 1)
    kpos = jnp.arange(s)

    def one_block(args):
        qi, start = args
        sc = jnp.einsum('bqhd,bkhd->bhqk', qi, k).astype(jnp.float32) * scale
        qpos = start + jnp.arange(Q_BLOCK)
        sc = jnp.where(kpos[None, :] <= qpos[:, None], sc, -jnp.inf)
        a = diff_combine(jax.nn.softmax(sc, axis=-1), lam).astype(v.dtype)
        return jnp.einsum('bhqk,bkhe->bqhe', a, v)

    o = lax.map(one_block, (qb, jnp.arange(nb) * Q_BLOCK))
    return o.swapaxes(0, 1).reshape(b, s, DA_HEADS, DA_V_DIM)


def diff_attention_sample(q, k, v, cache_k, cache_v, layer, page_table, lam):
    bd, t = q.shape[:2]
    scale = DA_HEAD_DIM ** -0.5
    k_past = cache_k[layer, page_table].reshape(bd, -1, DA_SUB_HEADS, DA_HEAD_DIM)
    v_past = cache_v[layer, page_table].reshape(bd, -1, DA_HEADS, DA_V_DIM)
    p_len = k_past.shape[1]
    s_past = jnp.einsum('bqhd,bkhd->bhqk', q, k_past).astype(jnp.float32) * scale
    s_new = jnp.einsum('bqhd,bkhd->bhqk', q, k).astype(jnp.float32) * scale
    causal = jnp.tril(jnp.ones((t, t), dtype=bool))
    s_new = jnp.where(causal, s_new, -jnp.inf)
    p = jax.nn.softmax(jnp.concatenate([s_past, s_new], axis=-1), axis=-1)
    a = diff_combine(p, lam).astype(v.dtype)
    return (jnp.einsum('bhqk,bkhe->bqhe', a[..., :p_len], v_past)
            + jnp.einsum('bhqk,bkhe->bqhe', a[..., p_len:], v))


def diff_attn_out(o, subln_w, lam_init, w_o):
    b, t = o.shape[:2]
    o = rms_norm(o, subln_w) * (1.0 - lam_init)
    return o.reshape(b, t, DA_HEADS * DA_V_DIM) @ w_o


def causal_conv(buf, x, w):
    t = x.shape[1]
    xin = jnp.concatenate([buf, x], axis=1)
    y = xin[:, 0:t] * w[0]
    for j in range(1, CONV_WIDTH):
        y = y + xin[:, j:j + t] * w[j]
    return jax.nn.silu(y), xin[:, t:]


def chunk_gated_delta_rule(q, k, v, g, beta, s0):
    b, t = q.shape[:2]
    pad = (-t) % CHUNK
    n = (t + pad) // CHUNK

    def to_chunks(x):
        x = jnp.pad(x, [(0, 0), (0, pad)] + [(0, 0)] * (x.ndim - 2))
        x = x.reshape((b, n, CHUNK) + x.shape[2:])
        return jnp.moveaxis(x, (1, 3), (0, 2))

    tril = jnp.tril(jnp.ones((CHUNK, CHUNK), dtype=bool))
    strict = jnp.tril(jnp.ones((CHUNK, CHUNK), dtype=bool), -1)
    eye = jnp.eye(CHUNK, dtype=jnp.float32)

    def chunk_step(s, xs):
        qc, kc, vc, gc, bc = xs
        gcum = jnp.cumsum(gc, axis=-1)
        decay = jnp.exp(jnp.where(tril, gcum[..., :, None] - gcum[..., None, :], -jnp.inf))
        kb = kc * bc[..., None]
        vb = vc * bc[..., None]
        m = jnp.where(strict, jnp.einsum('bhid,bhjd->bhij', kb, kc) * decay, 0.0)
        tinv = lax.linalg.triangular_solve(eye + m, jnp.broadcast_to(eye, m.shape),
                                           left_side=True, lower=True, unit_diagonal=True)
        w_val = tinv @ vb
        u = tinv @ (kb * jnp.exp(gcum)[..., None])
        v_new = w_val - u @ s
        intra = jnp.einsum('bhid,bhjd->bhij', qc, kc) * decay
        o = (qc * jnp.exp(gcum)[..., None]) @ s + intra @ v_new
        glast = gcum[..., -1]
        s_next = (s * jnp.exp(glast)[..., None, None]
                  + jnp.einsum('bhcd,bhce->bhde', kc * jnp.exp(glast[..., None] - gcum)[..., None], v_new))
        return s_next, o

    xs = (to_chunks(q), to_chunks(k), to_chunks(v), to_chunks(g), to_chunks(beta))
    s_fin, o = lax.scan(chunk_step, s0, xs)
    o = jnp.transpose(o, (1, 0, 3, 2, 4)).reshape(b, n * CHUNK, q.shape[2], v.shape[-1])[:, :t]
    return o, s_fin


def gated_deltanet(h, conv_buf, s0, w_in, conv_w, a_log, dt_bias, gnorm_w, w_o):
    f32 = jnp.float32
    b, t, _ = h.shape
    proj = h @ w_in
    qkv = proj[..., :GD_CONV_CH]
    z = proj[..., GD_CONV_CH:GD_CONV_CH + GD_VAL_WIDTH]
    beta_in = proj[..., GD_CONV_CH + GD_VAL_WIDTH:GD_CONV_CH + GD_VAL_WIDTH + GD_V_HEADS]
    a_in = proj[..., GD_CONV_CH + GD_VAL_WIDTH + GD_V_HEADS:]
    qkv, new_buf = causal_conv(conv_buf, qkv, conv_w)
    rep = GD_V_HEADS // GD_K_HEADS
    q = qkv[..., :GD_KEY_WIDTH].reshape(b, t, GD_K_HEADS, GD_K_DIM)
    k = qkv[..., GD_KEY_WIDTH:2 * GD_KEY_WIDTH].reshape(b, t, GD_K_HEADS, GD_K_DIM)
    v = qkv[..., 2 * GD_KEY_WIDTH:].reshape(b, t, GD_V_HEADS, GD_V_DIM).astype(f32)
    q = jnp.repeat(l2norm(q), rep, axis=2) * (GD_K_DIM ** -0.5)
    k = jnp.repeat(l2norm(k), rep, axis=2)
    beta = jax.nn.sigmoid(beta_in.astype(f32))
    g = -jnp.exp(a_log.astype(f32)) * jax.nn.softplus(a_in.astype(f32) + dt_bias.astype(f32))
    o, s_fin = chunk_gated_delta_rule(q, k, v, g, beta, s0.astype(f32))
    z = z.reshape(b, t, GD_V_HEADS, GD_V_DIM).astype(f32)
    o = rms_norm(o, gnorm_w) * jax.nn.silu(z)
    out = o.reshape(b, t, GD_VAL_WIDTH).astype(h.dtype) @ w_o
    return out, s_fin, new_buf


def swiglu(h, w_gu, w_down):
    gu = h @ w_gu
    f = w_down.shape[0]
    return (jax.nn.silu(gu[..., :f]) * gu[..., f:]) @ w_down


def moe_swiglu(h, w_router, w_gu, w_down):
    shp = h.shape
    hf = h.reshape(-1, shp[-1])
    probs = jax.nn.softmax((hf @ w_router).astype(jnp.float32), axis=-1)
    top_v, top_i = lax.top_k(probs, TOP_K)
    top_v = top_v / jnp.sum(top_v, axis=-1, keepdims=True)
    gates = jnp.sum(jax.nn.one_hot(top_i, N_EXPERTS, dtype=jnp.float32) * top_v[..., None], axis=-2)
    gates = gates.astype(h.dtype)
    out = jnp.zeros_like(hf)
    for e in range(N_EXPERTS):
        out = out + gates[:, e:e + 1] * swiglu(hf, w_gu[e], w_down[e])
    return out.reshape(shp)


def setup_inputs(seed: int = 0) -> dict:
    key = jax.random.key(seed)
    keys = iter(jax.random.split(key, 32))
    f32 = jnp.float32
    n_pages = PAST_LEN // PAGE_SIZE
    n_phys = (DEC_BATCH * n_pages * 5 + 3) // 4

    def nrm(shape, scale):
        return jax.random.normal(next(keys), shape, f32) * scale

    def gain(shape):
        return 1.0 + nrm(shape, 0.02)

    x_prompt = nrm((BATCH, SEQ, D_MODEL), 1.0)
    x_sample = nrm((DEC_BATCH, DEC_SEQ, D_MODEL), 1.0)
    cache_k = nrm((N_ATTN, n_phys, PAGE_SIZE, DA_SUB_HEADS, DA_HEAD_DIM), 1.0)
    cache_v = nrm((N_ATTN, n_phys, PAGE_SIZE, DA_HEADS, DA_V_DIM), 1.0)
    state_delta = nrm((N_DELTA, DEC_BATCH, GD_V_HEADS, GD_K_DIM, GD_V_DIM), 0.5)
    state_conv = nrm((N_DELTA, DEC_BATCH, CONV_WIDTH - 1, GD_CONV_CH), 1.0)
    page_table = jax.random.permutation(next(keys), n_phys)[:DEC_BATCH * n_pages]
    page_table = page_table.reshape(DEC_BATCH, n_pages).astype(jnp.int32)

    norm_mix = gain((DEPTH, D_MODEL))
    norm_ffn = gain((DEPTH, D_MODEL))
    norm_final = gain((D_MODEL,))

    w_qkv_da = nrm((N_ATTN, D_MODEL, DA_QKV_WIDTH), D_MODEL ** -0.5)
    lambda_q1 = nrm((N_ATTN, DA_HEAD_DIM), 0.1)
    lambda_k1 = nrm((N_ATTN, DA_HEAD_DIM), 0.1)
    lambda_q2 = nrm((N_ATTN, DA_HEAD_DIM), 0.1)
    lambda_k2 = nrm((N_ATTN, DA_HEAD_DIM), 0.1)
    subln_da = gain((N_ATTN, DA_V_DIM))
    w_o_da = nrm((N_ATTN, DA_HEADS * DA_V_DIM, D_MODEL), (DA_HEADS * DA_V_DIM) ** -0.5)

    w_in_gd = nrm((N_DELTA, D_MODEL, GD_IN_WIDTH), D_MODEL ** -0.5)
    conv_w_gd = nrm((N_DELTA, CONV_WIDTH, GD_CONV_CH), CONV_WIDTH ** -0.5)
    a_log_gd = jnp.log(jax.random.uniform(next(keys), (N_DELTA, GD_V_HEADS), f32, 1.0, 16.0))
    dt = jnp.exp(jax.random.uniform(next(keys), (N_DELTA, GD_V_HEADS), f32,
                                    math.log(1e-3), math.log(1e-1)))
    dt_bias_gd = dt + jnp.log(-jnp.expm1(-dt))
    gnorm_gd = gain((N_DELTA, GD_V_DIM))
    w_o_gd = nrm((N_DELTA, GD_VAL_WIDTH, D_MODEL), GD_VAL_WIDTH ** -0.5)

    w_gu_dense = nrm((N_ATTN, D_MODEL, 2 * D_FF), D_MODEL ** -0.5)
    w_down_dense = nrm((N_ATTN, D_FF, D_MODEL), D_FF ** -0.5)
    w_router = nrm((N_DELTA, D_MODEL, N_EXPERTS), D_MODEL ** -0.5)
    w_gu_moe = nrm((N_DELTA, N_EXPERTS, D_MODEL, 2 * D_FF_EXPERT), D_MODEL ** -0.5)
    w_down_moe = nrm((N_DELTA, N_EXPERTS, D_FF_EXPERT, D_MODEL), D_FF_EXPERT ** -0.5)

    return {
        'x_prompt': x_prompt, 'x_sample': x_sample,
        'cache_k': cache_k, 'cache_v': cache_v,
        'state_delta': state_delta, 'state_conv': state_conv,
        'page_table': page_table,
        'norm_mix': norm_mix, 'norm_ffn': norm_ffn, 'norm_final': norm_final,
        'w_qkv_da': w_qkv_da, 'lambda_q1': lambda_q1, 'lambda_k1': lambda_k1,
        'lambda_q2': lambda_q2, 'lambda_k2': lambda_k2, 'subln_da': subln_da, 'w_o_da': w_o_da,
        'w_in_gd': w_in_gd, 'conv_w_gd': conv_w_gd, 'a_log_gd': a_log_gd,
        'dt_bias_gd': dt_bias_gd, 'gnorm_gd': gnorm_gd, 'w_o_gd': w_o_gd,
        'w_gu_dense': w_gu_dense, 'w_down_dense': w_down_dense,
        'w_router': w_router, 'w_gu_moe': w_gu_moe, 'w_down_moe': w_down_moe,
    }


def reference(x_prompt, x_sample, cache_k, cache_v, state_delta, state_conv, page_table,
              norm_mix, norm_ffn, norm_final,
              w_qkv_da, lambda_q1, lambda_k1, lambda_q2, lambda_k2, subln_da, w_o_da,
              w_in_gd, conv_w_gd, a_log_gd, dt_bias_gd, gnorm_gd, w_o_gd,
              w_gu_dense, w_down_dense, w_router, w_gu_moe, w_down_moe):
    f32 = jnp.float32
    b, s = x_prompt.shape[:2]
    t = x_sample.shape[1]
    past = page_table.shape[1] * PAGE_SIZE
    pos_p = jnp.arange(s, dtype=f32)
    pos_s = past + jnp.arange(t, dtype=f32)
    xp, xs = x_prompt, x_sample
    kp_rows, vp_rows, ks_rows, vs_rows = [], [], [], []
    sp_fin, cp_fin, ss_fin, cs_fin = [], [], [], []
    for i in range(DEPTH):
        j = i // 2
        hp = rms_norm(xp, norm_mix[i])
        hs = rms_norm(xs, norm_mix[i])
        if i % 2 == 0:
            lam_init = LAMBDA_INIT_BASE - LAMBDA_INIT_AMP * math.exp(-LAMBDA_INIT_RATE * i)
            lam = diff_lambda(lambda_q1[j], lambda_k1[j], lambda_q2[j], lambda_k2[j], lam_init)
            qp, kp, vp = diff_attn_qkv(hp, w_qkv_da[j], pos_p)
            qs, ks, vs = diff_attn_qkv(hs, w_qkv_da[j], pos_s)
            op = diff_attention_prompt(qp, kp, vp, lam)
            o_s = diff_attention_sample(qs, ks, vs, cache_k, cache_v, j, page_table, lam)
            xp = xp + diff_attn_out(op, subln_da[j], lam_init, w_o_da[j])
            xs = xs + diff_attn_out(o_s, subln_da[j], lam_init, w_o_da[j])
            kp_rows.append(kp)
            vp_rows.append(vp)
            ks_rows.append(ks)
            vs_rows.append(vs)
            xp = xp + swiglu(rms_norm(xp, norm_ffn[i]), w_gu_dense[j], w_down_dense[j])
            xs = xs + swiglu(rms_norm(xs, norm_ffn[i]), w_gu_dense[j], w_down_dense[j])
        else:
            conv0 = jnp.zeros((b, CONV_WIDTH - 1, GD_CONV_CH), hp.dtype)
            s0 = jnp.zeros((b, GD_V_HEADS, GD_K_DIM, GD_V_DIM), f32)
            op, sp, cp = gated_deltanet(hp, conv0, s0, w_in_gd[j], conv_w_gd[j], a_log_gd[j],
                                        dt_bias_gd[j], gnorm_gd[j], w_o_gd[j])
            o_s, ss, cs = gated_deltanet(hs, state_conv[j], state_delta[j], w_in_gd[j], conv_w_gd[j],
                                         a_log_gd[j], dt_bias_gd[j], gnorm_gd[j], w_o_gd[j])
            xp = xp + op
            xs = xs + o_s
            sp_fin.append(sp)
            cp_fin.append(cp)
            ss_fin.append(ss)
            cs_fin.append(cs)
            xp = xp + moe_swiglu(rms_norm(xp, norm_ffn[i]), w_router[j], w_gu_moe[j], w_down_moe[j])
            xs = xs + moe_swiglu(rms_norm(xs, norm_ffn[i]), w_router[j], w_gu_moe[j], w_down_moe[j])
    y_prompt = rms_norm(xp, norm_final)
    y_sample = rms_norm(xs, norm_final)
    return (y_prompt, y_sample,
            jnp.stack(kp_rows), jnp.stack(vp_rows), jnp.stack(ks_rows), jnp.stack(vs_rows),
            jnp.stack(sp_fin), jnp.stack(cp_fin), jnp.stack(ss_fin), jnp.stack(cs_fin))
```

```python
import functools
import math

import numpy as np
import jax
import jax.numpy as jnp
from jax import lax
from jax.experimental import pallas as pl
from jax.experimental.pallas import tpu as pltpu

F32 = jnp.float32
BF16 = jnp.bfloat16

PAGE_SIZE = 128
DA_HEAD_DIM = 64
DA_V_DIM = 128
DA_HEADS = 16
DA_SUB_HEADS = 32
DA_QK_WIDTH = 2048
ROPE_THETA = 500000.0
ROT_DIM = 16
LAMBDA_INIT_BASE = 0.8
LAMBDA_INIT_AMP = 0.6
LAMBDA_INIT_RATE = 0.3
GD_K_HEADS = 16
GD_V_HEADS = 32
GD_HEAD_DIM = 128
GD_KEY_WIDTH = 2048
GD_VAL_WIDTH = 4096
GD_CONV_CH = 8192
CONV_WIDTH = 4
CHUNK = 64
N_EXPERTS = 8
NORM_EPS = 1e-6
L2_EPS = 1e-6

LANES = 128
SUBLANES = 8
VMEM_LIMIT_BYTES = 56 * 1024 * 1024


def _params(n_grid_dims, vmem=VMEM_LIMIT_BYTES):
    return pltpu.CompilerParams(dimension_semantics=("arbitrary",) * n_grid_dims,
                                vmem_limit_bytes=vmem)


def _sigmoid(x):
    return 1.0 / (1.0 + jnp.exp(-x))


def _silu(x):
    return x * _sigmoid(x)


def _dot(a, b):
    return jnp.dot(a, b, preferred_element_type=F32)


def _dot_nt(a, b):
    return lax.dot_general(a, b, (((1,), (1,)), ((), ())), preferred_element_type=F32)


def _dot_tn(a, b):
    return lax.dot_general(a, b, (((0,), (0,)), ((), ())), preferred_element_type=F32)


def _dot_f32(a, b):
    return jnp.dot(a, b, preferred_element_type=F32, precision=lax.Precision.HIGHEST)


def _split_bf16(x):
    hi = x.astype(BF16)
    return hi, (x - hi.astype(F32)).astype(BF16)


def _rmsnorm_kernel(x_ref, w_ref, o_ref):
    x = x_ref[...]
    ms = jnp.mean(x * x, axis=-1, keepdims=True)
    o_ref[...] = (x * lax.rsqrt(ms + NORM_EPS) * w_ref[...]).astype(o_ref.dtype)


def _rmsnorm(x, w, out_dtype, tm):
    m, d = x.shape
    return pl.pallas_call(
        _rmsnorm_kernel,
        grid=(m // tm,),
        in_specs=[pl.BlockSpec((tm, d), lambda i: (i, 0)),
                  pl.BlockSpec((1, d), lambda i: (0, 0))],
        out_specs=pl.BlockSpec((tm, d), lambda i: (i, 0)),
        out_shape=jax.ShapeDtypeStruct((m, d), out_dtype),
        compiler_params=_params(1),
        name="rmsnorm",
    )(x, w.reshape(1, d))


def _mm_kernel(*refs, n_w, n_x, n_o, epilogue, precise):
    a_ref = refs[0]
    w_refs = refs[1:1 + n_w]
    x_refs = refs[1 + n_w:1 + n_w + n_x]
    o_refs = refs[1 + n_w + n_x:1 + n_w + n_x + n_o]
    wbf_refs = refs[1 + n_w + n_x + n_o:]

    a = a_ref[...]
    if precise:
        accs = [_dot_f32(a, w_ref[...]) for w_ref in w_refs]
    else:
        @pl.when(pl.program_id(1) == 0)
        def _():
            for w_ref, wbf_ref in zip(w_refs, wbf_refs):
                wbf_ref[...] = w_ref[...].astype(BF16)

        accs = [_dot(a, wbf_ref[...]) for wbf_ref in wbf_refs]
    for o_ref, val in zip(o_refs, epilogue(accs, x_refs)):
        o_ref[...] = val.astype(o_ref.dtype)


def _matmul(a, w, widx, col_starts, n_cols, tm, tn, epilogue, extras, out_dtypes, name):
    m, k = a.shape
    precise = a.dtype == F32
    in_specs = [pl.BlockSpec((tm, k), lambda j, i: (i, 0))]
    for cs in col_starts:
        in_specs.append(pl.BlockSpec((None, k, tn), lambda j, i, cb=cs // tn: (widx, 0, cb + j)))
    for _, bs, im in extras:
        in_specs.append(pl.BlockSpec(bs, im))
    kernel = functools.partial(_mm_kernel, n_w=len(col_starts), n_x=len(extras),
                               n_o=len(out_dtypes), epilogue=epilogue, precise=precise)
    outs = pl.pallas_call(
        kernel,
        grid=(n_cols // tn, m // tm),
        in_specs=in_specs,
        out_specs=[pl.BlockSpec((tm, tn), lambda j, i: (i, j)) for _ in out_dtypes],
        out_shape=[jax.ShapeDtypeStruct((m, n_cols), dt) for dt in out_dtypes],
        scratch_shapes=[] if precise else [pltpu.VMEM((k, tn), BF16) for _ in col_starts],
        compiler_params=_params(2),
        name=name,
    )(a, *([w] * len(col_starts)), *[e[0] for e in extras])
    return outs


def _epi_plain(n_out):
    return lambda accs, xs: [accs[0]] * n_out


def _epi_residual(accs, xs):
    return [xs[0][...] + accs[0]]


def _epi_swiglu(accs, xs):
    return [_silu(accs[0]) * accs[1]]


def _epi_gated_residual(expert):
    def epi(accs, xs):
        gate = xs[1][...][:, expert:expert + 1]
        return [xs[0][...] + gate * accs[0]]
    return epi


def _apply_rope(acc, cos_ref, sin_ref):
    cos = cos_ref[...]
    sin = sin_ref[...]
    lane = lax.broadcasted_iota(jnp.int32, (1, LANES), 1) % DA_HEAD_DIM
    first_half = lane < ROT_DIM // 2
    cols = []
    for c in range(acc.shape[1] // LANES):
        x = acc[:, c * LANES:(c + 1) * LANES]
        partner = jnp.where(first_half, pltpu.roll(x, LANES - ROT_DIM // 2, 1),
                            pltpu.roll(x, ROT_DIM // 2, 1))
        cols.append(x * cos + partner * sin)
    return jnp.concatenate(cols, axis=1)


def _epi_rope(scales):
    def epi(accs, xs):
        r = _apply_rope(accs[0], xs[0], xs[1])
        return [r if s == 1.0 else r * s for s in scales]
    return epi


def _rope_table_kernel(inv_ref, sgn_ref, cos_ref, sin_ref, *, offset, period):
    rows = cos_ref.shape[0]
    t = lax.broadcasted_iota(jnp.int32, (rows, LANES), 0) % period
    pos = (t + offset).astype(F32)
    ang = pos * inv_ref[...]
    sgn = sgn_ref[...]
    cos_ref[...] = jnp.where(sgn != 0.0, jnp.cos(ang), 1.0)
    sin_ref[...] = sgn * jnp.sin(ang)


def _rope_tables(rows, offset, period):
    half = ROT_DIM // 2
    inv = ROPE_THETA ** (-2.0 * np.arange(half, dtype=np.float32) / ROT_DIM)
    lane = np.arange(LANES) % DA_HEAD_DIM
    inv_lane = np.where(lane < ROT_DIM, inv[lane % half], 0.0).astype(np.float32)
    sgn_lane = np.where(lane < half, -1.0, np.where(lane < ROT_DIM, 1.0, 0.0)).astype(np.float32)
    kernel = functools.partial(_rope_table_kernel, offset=offset, period=period)
    return pl.pallas_call(
        kernel,
        out_shape=[jax.ShapeDtypeStruct((rows, LANES), F32)] * 2,
        name="rope_tables",
    )(jnp.asarray(inv_lane).reshape(1, LANES), jnp.asarray(sgn_lane).reshape(1, LANES))


def _diff_lambda(lam_ref, lam_init):
    lv = lam_ref[...]
    a = jnp.sum(lv[0:1] * lv[1:2], axis=-1, keepdims=True)
    b = jnp.sum(lv[2:3] * lv[3:4], axis=-1, keepdims=True)
    return jnp.exp(a) - jnp.exp(b) + lam_init


def _sub_layer_norm(o, subln_ref, lam_init):
    ms = jnp.mean(o * o, axis=-1, keepdims=True)
    return o * lax.rsqrt(ms + NORM_EPS) * subln_ref[...] * (1.0 - lam_init)


def _attn_prompt_kernel(lam_ref, subln_ref, q_ref, k_ref, v_ref, o_ref, m_sc, l_sc, acc_sc,
                        *, tq, tk, lam_init):
    qi = pl.program_id(2)
    ki = pl.program_id(3)

    @pl.when(ki == 0)
    def _():
        m_sc[...] = jnp.full(m_sc.shape, -jnp.inf, F32)
        l_sc[...] = jnp.zeros(l_sc.shape, F32)
        acc_sc[...] = jnp.zeros(acc_sc.shape, F32)

    @pl.when(ki <= qi)
    def _():
        q = q_ref[...]
        k = k_ref[...]
        v = v_ref[...]
        lane = lax.broadcasted_iota(jnp.int32, (1, LANES), 1)
        row = lax.broadcasted_iota(jnp.int32, (tq, tk), 0) + qi * tq
        col = lax.broadcasted_iota(jnp.int32, (tq, tk), 1) + ki * tk
        causal = col <= row
        zero = jnp.zeros_like(q)
        for sub in range(2):
            in_sub = (lane < DA_HEAD_DIM) if sub == 0 else (lane >= DA_HEAD_DIM)
            s = _dot_nt(jnp.where(in_sub, q, zero), k)
            s = jnp.where(causal, s, -jnp.inf)
            m_prev = m_sc[sub]
            m_new = jnp.maximum(m_prev, jnp.max(s, axis=-1, keepdims=True))
            alpha = jnp.exp(m_prev - m_new)
            p = jnp.exp(s - m_new[:, :1])
            l_sc[sub] = alpha * l_sc[sub] + jnp.sum(p, axis=-1, keepdims=True)
            acc_sc[sub] = alpha * acc_sc[sub] + _dot(p.astype(BF16), v)
            m_sc[sub] = m_new

    @pl.when(ki == qi)
    def _():
        lam = _diff_lambda(lam_ref, lam_init)
        o = acc_sc[0] / l_sc[0] - lam * (acc_sc[1] / l_sc[1])
        o_ref[...] = _sub_layer_norm(o, subln_ref, lam_init).astype(o_ref.dtype)


def _attn_prompt(q, k, v, lam_vecs, subln, batch, seq, lam_init, tq=512):
    nq = seq // tq
    kernel = functools.partial(_attn_prompt_kernel, tq=tq, tk=tq, lam_init=lam_init)
    kv_spec = pl.BlockSpec((tq, LANES), lambda b, h, qi, ki: (b * nq + jnp.minimum(ki, qi), h))
    return pl.pallas_call(
        kernel,
        grid=(batch, DA_HEADS, nq, nq),
        in_specs=[pl.BlockSpec((SUBLANES, LANES), lambda b, h, qi, ki: (0, 0)),
                  pl.BlockSpec((1, LANES), lambda b, h, qi, ki: (0, 0)),
                  pl.BlockSpec((tq, LANES), lambda b, h, qi, ki: (b * nq + qi, h)),
                  kv_spec, kv_spec],
        out_specs=pl.BlockSpec((tq, LANES), lambda b, h, qi, ki: (b * nq + qi, h)),
        out_shape=jax.ShapeDtypeStruct((batch * seq, DA_HEADS * DA_V_DIM), BF16),
        scratch_shapes=[pltpu.VMEM((2, tq, LANES), F32), pltpu.VMEM((2, tq, LANES), F32),
                        pltpu.VMEM((2, tq, LANES), F32)],
        compiler_params=_params(4),
        name="attn_prompt",
    )(lam_vecs, subln, q, k, v)


def _attn_decode_kernel(pt_ref, lam_ref, subln_ref, q_ref, k_ref, v_ref, kn_ref, vn_ref, o_ref,
                        m_sc, l_sc, acc_sc, *, lam_init):
    del pt_ref
    p = pl.program_id(1)

    @pl.when(p == 0)
    def _():
        m_sc[...] = jnp.full(m_sc.shape, -jnp.inf, F32)
        l_sc[...] = jnp.zeros(l_sc.shape, F32)
        acc_sc[...] = jnp.zeros(acc_sc.shape, F32)

    def update(k_src, v_src, n_pos, n_valid):
        width = n_pos * DA_HEADS
        parts = []
        for parity in range(2):
            kb = k_src[:, pl.ds(parity, DA_HEADS, stride=2), :]
            k_hi, k_lo = _split_bf16(kb.reshape(width, DA_HEAD_DIM))
            q_hi, q_lo = _split_bf16(q_ref[parity])
            both = _dot_nt(jnp.concatenate([q_hi, q_lo], axis=0), k_hi)
            parts.append(both[:DA_HEADS] + both[DA_HEADS:] + _dot_nt(q_hi, k_lo))
        s = jnp.concatenate(parts, axis=0)
        row = lax.broadcasted_iota(jnp.int32, s.shape, 0) % DA_HEADS
        lane = lax.broadcasted_iota(jnp.int32, s.shape, 1)
        own = (lane % DA_HEADS == row) & (lane < n_valid * DA_HEADS)
        s = jnp.where(own, s, -jnp.inf)
        m_prev = m_sc[...]
        m_new = jnp.maximum(m_prev, jnp.max(s, axis=-1, keepdims=True))
        alpha = jnp.exp(m_prev - m_new)
        pe = jnp.exp(s - m_new[:, :1])
        l_sc[...] = alpha * l_sc[...] + jnp.sum(pe, axis=-1, keepdims=True)
        v_hi, v_lo = _split_bf16(v_src[...].reshape(width, DA_V_DIM))
        p_hi, p_lo = _split_bf16(pe)
        both = _dot(jnp.concatenate([p_hi, p_lo], axis=0), v_hi)
        pv = both[:DA_SUB_HEADS] + both[DA_SUB_HEADS:] + _dot(p_hi, v_lo)
        acc_sc[...] = alpha * acc_sc[...] + pv
        m_sc[...] = m_new

    update(k_ref, v_ref, PAGE_SIZE, PAGE_SIZE)

    @pl.when(p == pl.num_programs(1) - 1)
    def _():
        update(kn_ref, vn_ref, SUBLANES, 1)
        lam = _diff_lambda(lam_ref, lam_init)
        o = acc_sc[...] / l_sc[...]
        o = o[:DA_HEADS] - lam * o[DA_HEADS:]
        o_ref[...] = _sub_layer_norm(o, subln_ref, lam_init).astype(o_ref.dtype)


def _attn_decode(q, cache_k, cache_v, layer, page_table, k_new, v_new, lam_vecs, subln, lam_init):
    bd, n_pages = page_table.shape
    kernel = functools.partial(_attn_decode_kernel, lam_init=lam_init)
    grid_spec = pltpu.PrefetchScalarGridSpec(
        num_scalar_prefetch=1,
        grid=(bd, n_pages),
        in_specs=[
            pl.BlockSpec((SUBLANES, LANES), lambda b, p, pt: (0, 0)),
            pl.BlockSpec((1, LANES), lambda b, p, pt: (0, 0)),
            pl.BlockSpec((None, 2, DA_HEADS, DA_HEAD_DIM), lambda b, p, pt: (b, 0, 0, 0)),
            pl.BlockSpec((None, None, PAGE_SIZE, DA_SUB_HEADS, DA_HEAD_DIM),
                         lambda b, p, pt: (layer, pt[b, p], 0, 0, 0)),
            pl.BlockSpec((None, None, PAGE_SIZE, DA_HEADS, DA_V_DIM),
                         lambda b, p, pt: (layer, pt[b, p], 0, 0, 0)),
            pl.BlockSpec((None, SUBLANES, DA_SUB_HEADS, DA_HEAD_DIM), lambda b, p, pt: (b, 0, 0, 0)),
            pl.BlockSpec((None, SUBLANES, DA_HEADS, DA_V_DIM), lambda b, p, pt: (b, 0, 0, 0)),
        ],
        out_specs=pl.BlockSpec((None, DA_HEADS, DA_V_DIM), lambda b, p, pt: (b, 0, 0)),
        scratch_shapes=[pltpu.VMEM((DA_SUB_HEADS, LANES), F32), pltpu.VMEM((DA_SUB_HEADS, LANES), F32),
                        pltpu.VMEM((DA_SUB_HEADS, DA_V_DIM), F32)],
    )
    return pl.pallas_call(
        kernel,
        grid_spec=grid_spec,
        out_shape=jax.ShapeDtypeStruct((bd, DA_HEADS, DA_V_DIM), F32),
        compiler_params=_params(2),
        name="attn_decode",
    )(page_table, lam_vecs, subln, q, cache_k, cache_v, k_new, v_new)


def _gdn_gates_kernel(ba_ref, alog_ref, dtb_ref, o_ref):
    x = ba_ref[...]
    lane = lax.broadcasted_iota(jnp.int32, x.shape, 1)
    beta = _sigmoid(x)
    y = x + dtb_ref[...]
    softplus = jnp.maximum(y, 0.0) + jnp.log1p(jnp.exp(-jnp.abs(y)))
    g = -jnp.exp(alog_ref[...]) * softplus
    o_ref[...] = jnp.where(lane < GD_V_HEADS, beta, jnp.where(lane < 2 * GD_V_HEADS, g, 0.0))


def _gdn_gates(ba, a_log, dt_bias, tm):
    m = ba.shape[0]
    pad = lambda v: jnp.pad(v.reshape(1, GD_V_HEADS), ((0, 0), (GD_V_HEADS, LANES - 2 * GD_V_HEADS)))
    return pl.pallas_call(
        _gdn_gates_kernel,
        grid=(m // tm,),
        in_specs=[pl.BlockSpec((tm, LANES), lambda i: (i, 0)),
                  pl.BlockSpec((1, LANES), lambda i: (0, 0)),
                  pl.BlockSpec((1, LANES), lambda i: (0, 0))],
        out_specs=pl.BlockSpec((tm, LANES), lambda i: (i, 0)),
        out_shape=jax.ShapeDtypeStruct((m, LANES), F32),
        compiler_params=_params(1),
        name="gdn_gates",
    )(ba, pad(a_log), pad(dt_bias))


def _gdn_conv_kernel(x_ref, buf_ref, w_ref, o_ref, xin_sc):
    c = pl.program_id(1)
    t = x_ref.shape[0]
    x = x_ref[...]
    w = w_ref[...]
    xin_sc[0:SUBLANES, :] = buf_ref[...]
    xin_sc[SUBLANES:SUBLANES + t, :] = x
    y = x * w[CONV_WIDTH - 1:CONV_WIDTH]
    for s in range(1, CONV_WIDTH):
        shifted = xin_sc[pl.ds(SUBLANES - s, t), :]
        y = y + shifted * w[CONV_WIDTH - 1 - s:CONV_WIDTH - s]
    y = _silu(y)
    inv = lax.rsqrt(jnp.sum(y * y, axis=-1, keepdims=True) + L2_EPS)
    scale = jnp.where(c < GD_K_HEADS, GD_HEAD_DIM ** -0.5, 1.0)
    o_ref[...] = jnp.where(c < 2 * GD_K_HEADS, y * (inv * scale), y)


def _gdn_conv(x, buf, conv_w, widx, batch, t):
    n_blocks = GD_CONV_CH // LANES
    return pl.pallas_call(
        _gdn_conv_kernel,
        grid=(batch, n_blocks),
        in_specs=[pl.BlockSpec((t, LANES), lambda b, c: (b, c)),
                  pl.BlockSpec((SUBLANES, LANES), lambda b, c: (b, c)),
                  pl.BlockSpec((None, CONV_WIDTH, LANES), lambda b, c: (widx, 0, c))],
        out_specs=pl.BlockSpec((t, LANES), lambda b, c: (b, c)),
        out_shape=jax.ShapeDtypeStruct((batch * t, GD_CONV_CH), F32),
        scratch_shapes=[pltpu.VMEM((SUBLANES + t, LANES), F32)],
        compiler_params=_params(2),
        name="gdn_conv",
    )(x, buf, conv_w)


def _unit_lower_inverse(m_strict, eye):
    inv = eye - m_strict
    power = _dot_f32(m_strict, m_strict)
    n_factors = int(math.log2(CHUNK)) - 1
    for f in range(n_factors):
        inv = inv + _dot_f32(inv, power)
        if f + 1 < n_factors:
            power = _dot_f32(power, power)
    return inv


def _gdn_chunk_kernel(q_ref, k_ref, v_ref, z_ref, bg_ref, gt_ref, gn_ref, s0_ref, o_ref, sfin_ref,
                      s_sc, *, heads):
    grp = pl.program_id(1)
    c = pl.program_id(2)

    @pl.when(c == 0)
    def _():
        s_sc[...] = s0_ref[...]

    bg = bg_ref[...]
    lane = lax.broadcasted_iota(jnp.int32, (1, LANES), 1)
    ri = lax.broadcasted_iota(jnp.int32, (CHUNK, CHUNK), 0)
    ci = lax.broadcasted_iota(jnp.int32, (CHUNK, CHUNK), 1)
    tril = ri >= ci
    strict = ri > ci
    eye = (ri == ci).astype(F32)
    gn = gn_ref[...]

    for kh in range(heads // 2):
        qc = q_ref[:, kh * LANES:(kh + 1) * LANES]
        kc = k_ref[:, kh * LANES:(kh + 1) * LANES]
        kc_bf = kc.astype(BF16)
        kk = _dot_nt(kc_bf, kc_bf)
        qk = _dot_nt(qc.astype(BF16), kc_bf)
        for hh in (2 * kh, 2 * kh + 1):
            h = grp * heads + hh
            vc = v_ref[:, hh * LANES:(hh + 1) * LANES]
            beta = jnp.sum(jnp.where(lane == h, bg, 0.0), axis=-1, keepdims=True)
            g_col = jnp.sum(jnp.where(lane == h + GD_V_HEADS, bg, 0.0), axis=-1, keepdims=True)
            g_row = gt_ref[hh:hh + 1, :]
            gcum_col = jnp.sum(jnp.where(tril, g_row, 0.0), axis=-1, keepdims=True)
            gcum_row = jnp.sum(jnp.where(ri <= ci, g_col, 0.0), axis=0, keepdims=True)
            decay = jnp.exp(jnp.where(tril, gcum_col - gcum_row, -jnp.inf))
            tinv = _unit_lower_inverse(jnp.where(strict, beta * kk * decay, 0.0), eye)
            eg = jnp.exp(gcum_col)
            rhs = jnp.concatenate([vc * beta, kc * (beta * eg)], axis=1).astype(BF16)
            wu = _dot(tinv.astype(BF16), rhs)
            s_old = s_sc[hh]
            s_bf = s_old.astype(BF16)
            v_new = wu[:, :LANES] - _dot(wu[:, LANES:].astype(BF16), s_bf)
            v_new_bf = v_new.astype(BF16)
            o = _dot((qc * eg).astype(BF16), s_bf) + _dot((qk * decay).astype(BF16), v_new_bf)
            g_last = gcum_col[CHUNK - 1:CHUNK, :]
            k_dec = (kc * jnp.exp(g_last - gcum_col)).astype(BF16)
            s_sc[hh] = s_old * jnp.exp(g_last) + _dot_tn(k_dec, v_new_bf)
            z = z_ref[:, hh * LANES:(hh + 1) * LANES]
            ms = jnp.mean(o * o, axis=-1, keepdims=True)
            o_ref[:, hh * LANES:(hh + 1) * LANES] = (
                o * lax.rsqrt(ms + NORM_EPS) * gn * _silu(z)).astype(o_ref.dtype)

    @pl.when(c == pl.num_programs(2) - 1)
    def _():
        sfin_ref[...] = s_sc[...]


def _gdn_chunks(qkv, proj, bg, g_t, gnorm, s0, batch, t, heads=8):
    nc = t // CHUNK
    n_grp = GD_V_HEADS // heads
    kw = heads // 2 * LANES
    vw = heads * LANES
    row = lambda b, g, c: b * nc + c
    kernel = functools.partial(_gdn_chunk_kernel, heads=heads)
    return pl.pallas_call(
        kernel,
        grid=(batch, n_grp, nc),
        in_specs=[
            pl.BlockSpec((CHUNK, kw), lambda b, g, c: (row(b, g, c), g)),
            pl.BlockSpec((CHUNK, kw), lambda b, g, c: (row(b, g, c), GD_KEY_WIDTH // kw + g)),
            pl.BlockSpec((CHUNK, vw), lambda b, g, c: (row(b, g, c), 2 * GD_KEY_WIDTH // vw + g)),
            pl.BlockSpec((CHUNK, vw), lambda b, g, c: (row(b, g, c), GD_CONV_CH // vw + g)),
            pl.BlockSpec((CHUNK, LANES), lambda b, g, c: (row(b, g, c), 0)),
            pl.BlockSpec((None, None, heads, CHUNK), lambda b, g, c: (b, c, g, 0)),
            pl.BlockSpec((1, LANES), lambda b, g, c: (0, 0)),
            pl.BlockSpec((None, heads, GD_HEAD_DIM, GD_HEAD_DIM), lambda b, g, c: (b, g, 0, 0)),
        ],
        out_specs=[
            pl.BlockSpec((CHUNK, vw), lambda b, g, c: (row(b, g, c), g)),
            pl.BlockSpec((None, heads, GD_HEAD_DIM, GD_HEAD_DIM), lambda b, g, c: (b, g, 0, 0)),
        ],
        out_shape=[jax.ShapeDtypeStruct((batch * t, GD_VAL_WIDTH), BF16),
                   jax.ShapeDtypeStruct((batch, GD_V_HEADS, GD_HEAD_DIM, GD_HEAD_DIM), F32)],
        scratch_shapes=[pltpu.VMEM((heads, GD_HEAD_DIM, GD_HEAD_DIM), F32)],
        compiler_params=_params(3),
        name="gdn_chunks",
    )(qkv, qkv, qkv, proj, bg, g_t, gnorm, s0)


def _gdn_step_kernel(qkv_ref, z_ref, bg_ref, gn_ref, s0_ref, o_ref, sfin_ref):
    ri = lax.broadcasted_iota(jnp.int32, (GD_HEAD_DIM, GD_HEAD_DIM), 0)
    ci = lax.broadcasted_iota(jnp.int32, (GD_HEAD_DIM, GD_HEAD_DIM), 1)
    diag = ri == ci
    bg = bg_ref[...]
    gn = gn_ref[...]

    def column(row_vec):
        return jnp.sum(jnp.where(diag, row_vec, 0.0), axis=-1, keepdims=True)

    for kh in range(GD_K_HEADS):
        q_col = column(qkv_ref[:, kh * LANES:(kh + 1) * LANES])
        k_col = column(qkv_ref[:, GD_KEY_WIDTH + kh * LANES:GD_KEY_WIDTH + (kh + 1) * LANES])
        for hh in (2 * kh, 2 * kh + 1):
            v = qkv_ref[:, 2 * GD_KEY_WIDTH + hh * LANES:2 * GD_KEY_WIDTH + (hh + 1) * LANES]
            beta = bg[:, hh:hh + 1]
            eg = jnp.exp(bg[:, GD_V_HEADS + hh:GD_V_HEADS + hh + 1])
            s_old = s0_ref[hh]
            sk = jnp.sum(s_old * k_col, axis=0, keepdims=True)
            v_new = beta * (v - eg * sk)
            s_new = eg * s_old + k_col * v_new
            sfin_ref[hh] = s_new
            o = jnp.sum(s_new * q_col, axis=0, keepdims=True)
            z = z_ref[:, hh * LANES:(hh + 1) * LANES]
            ms = jnp.mean(o * o, axis=-1, keepdims=True)
            o_ref[:, hh * LANES:(hh + 1) * LANES] = o * lax.rsqrt(ms + NORM_EPS) * gn * _silu(z)


def _gdn_step(qkv, z, bg, gnorm, s0):
    batch = qkv.shape[0]
    row = lambda width: pl.BlockSpec((None, 1, width), lambda b: (b, 0, 0))
    state = pl.BlockSpec((None, GD_V_HEADS, GD_HEAD_DIM, GD_HEAD_DIM), lambda b: (b, 0, 0, 0))
    return pl.pallas_call(
        _gdn_step_kernel,
        grid=(batch,),
        in_specs=[row(GD_CONV_CH), row(GD_VAL_WIDTH), row(LANES),
                  pl.BlockSpec((1, LANES), lambda b: (0, 0)), state],
        out_specs=[row(GD_VAL_WIDTH), state],
        out_shape=[jax.ShapeDtypeStruct((batch, 1, GD_VAL_WIDTH), F32),
                   jax.ShapeDtypeStruct(s0.shape, F32)],
        compiler_params=_params(1),
        name="gdn_step",
    )(qkv, z, bg, gnorm, s0)


def _router_kernel(h_ref, w_ref, o_ref):
    if h_ref.dtype == F32:
        logits = _dot_f32(h_ref[...], w_ref[...])
    else:
        logits = _dot(h_ref[...], w_ref[...].astype(BF16))
    e = jnp.exp(logits - jnp.max(logits, axis=-1, keepdims=True))
    probs = e / jnp.sum(e, axis=-1, keepdims=True)
    lane = lax.broadcasted_iota(jnp.int32, probs.shape, 1)
    v1 = jnp.max(probs, axis=-1, keepdims=True)
    i1 = jnp.min(jnp.where(probs == v1, lane, N_EXPERTS), axis=-1, keepdims=True)
    rest = jnp.where(lane == i1, -1.0, probs)
    v2 = jnp.max(rest, axis=-1, keepdims=True)
    i2 = jnp.min(jnp.where(rest == v2, lane, N_EXPERTS), axis=-1, keepdims=True)
    total = v1 + v2
    o_ref[...] = jnp.where(lane == i1, v1 / total, 0.0) + jnp.where(lane == i2, v2 / total, 0.0)


def _router(h, w_router, widx, tm):
    m, d = h.shape
    return pl.pallas_call(
        _router_kernel,
        grid=(m // tm,),
        in_specs=[pl.BlockSpec((tm, d), lambda i: (i, 0)),
                  pl.BlockSpec((None, d, N_EXPERTS), lambda i: (widx, 0, 0))],
        out_specs=pl.BlockSpec((tm, N_EXPERTS), lambda i: (i, 0)),
        out_shape=jax.ShapeDtypeStruct((m, N_EXPERTS), F32),
        compiler_params=_params(1),
        name="moe_router",
    )(h, w_router)


def _lambda_vectors(lq1, lk1, lq2, lk2):
    rows = jnp.stack([lq1, lk1, lq2, lk2]).astype(F32)
    return jnp.pad(rows, ((0, SUBLANES - 4), (0, LANES - DA_HEAD_DIM)))


def _attention_layer(x, tm, layer, j, w, rope, attend, decode):
    d = x.shape[1]
    act = F32 if decode else BF16
    lam_init = LAMBDA_INIT_BASE - LAMBDA_INIT_AMP * math.exp(-LAMBDA_INIT_RATE * layer)
    lam_vecs = _lambda_vectors(w['lambda_q1'][j], w['lambda_k1'][j], w['lambda_q2'][j], w['lambda_k2'][j])
    subln = w['subln_da'][j].reshape(1, DA_V_DIM)
    cos, sin, n_tab = rope
    tabs = [(arr, (tm, LANES), lambda jj, i: (i % n_tab, 0)) for arr in (cos, sin)]
    h = _rmsnorm(x, w['norm_mix'][layer], act, tm)
    tn = 512
    kv_dtypes = [F32] if decode else [F32, BF16]
    (q,) = _matmul(h, w['w_qkv_da'], j, [0], DA_QK_WIDTH, tm, tn,
                   _epi_rope([DA_HEAD_DIM ** -0.5]), tabs, [act], "da_q")
    k = _matmul(h, w['w_qkv_da'], j, [DA_QK_WIDTH], DA_QK_WIDTH, tm, tn,
                _epi_rope([1.0] * len(kv_dtypes)), tabs, kv_dtypes, "da_k")
    v = _matmul(h, w['w_qkv_da'], j, [2 * DA_QK_WIDTH], DA_HEADS * DA_V_DIM, tm, tn,
                _epi_plain(len(kv_dtypes)), [], kv_dtypes, "da_v")
    o = attend(q, k[-1], v[-1], lam_vecs, subln, lam_init)
    res = (x, (tm, tn), lambda jj, i: (i, jj))
    (x,) = _matmul(o, w['w_o_da'], j, [0], d, tm, tn, _epi_residual, [res], [F32], "da_out")
    h = _rmsnorm(x, w['norm_ffn'][layer], act, tm)
    d_ff = w['w_down_dense'].shape[1]
    (hid,) = _matmul(h, w['w_gu_dense'], j, [0, d_ff], d_ff, tm, tn, _epi_swiglu, [], [act], "ffn_gu")
    tm_down = min(tm, 256)
    res = (x, (tm_down, tn), lambda jj, i: (i, jj))
    (x,) = _matmul(hid, w['w_down_dense'], j, [0], d, tm_down, tn, _epi_residual, [res], [F32], "ffn_down")
    return x, k[0], v[0]


def _deltanet_layer(x, tm, layer, j, w, batch, t, conv_buf, s0, decode):
    d = x.shape[1]
    act = F32 if decode else BF16
    h = _rmsnorm(x, w['norm_mix'][layer], act, tm)
    n_main = GD_CONV_CH + GD_VAL_WIDTH
    (proj,) = _matmul(h, w['w_in_gd'], j, [0], n_main, tm, 512, _epi_plain(1), [], [F32], "gd_in")
    (ba,) = _matmul(h, w['w_in_gd'], j, [n_main], LANES, tm, LANES, _epi_plain(1), [], [F32], "gd_in_ba")
    bg = _gdn_gates(ba, w['a_log_gd'][j], w['dt_bias_gd'][j], tm)
    gnorm = w['gnorm_gd'][j].reshape(1, GD_HEAD_DIM)
    if decode:
        conv_in = jnp.pad(proj.reshape(batch, 1, -1), ((0, 0), (0, SUBLANES - 1), (0, 0)))
        qkv = _gdn_conv(conv_in.reshape(batch * SUBLANES, -1), conv_buf, w['conv_w_gd'], j, batch, SUBLANES)
        qkv = qkv.reshape(batch, SUBLANES, -1)[:, :1]
        z = proj[:, GD_CONV_CH:n_main].reshape(batch, 1, GD_VAL_WIDTH)
        o, s_fin = _gdn_step(qkv, z, bg.reshape(batch, 1, LANES), gnorm, s0)
        o = o.reshape(batch, GD_VAL_WIDTH)
    else:
        qkv = _gdn_conv(proj, conv_buf, w['conv_w_gd'], j, batch, t)
        g_t = bg[:, GD_V_HEADS:2 * GD_V_HEADS].reshape(batch, t // CHUNK, CHUNK, GD_V_HEADS)
        g_t = jnp.swapaxes(g_t, 2, 3)
        o, s_fin = _gdn_chunks(qkv, proj, bg, g_t, gnorm, s0, batch, t)
    res = (x, (tm, 512), lambda jj, i: (i, jj))
    (x,) = _matmul(o, w['w_o_gd'], j, [0], d, tm, 512, _epi_residual, [res], [F32], "gd_out")
    h = _rmsnorm(x, w['norm_ffn'][layer], act, tm)
    gates = _router(h, w['w_router'], j, tm)
    h = h.astype(BF16)
    w_gu = w['w_gu_moe'][j]
    w_down = w['w_down_moe'][j]
    d_ffe = w_down.shape[1]
    gate_x = (gates, (tm, N_EXPERTS), lambda jj, i: (i, 0))
    for e in range(N_EXPERTS):
        (hid,) = _matmul(h, w_gu, e, [0, d_ffe], d_ffe, tm, 256, _epi_swiglu, [], [BF16], "moe_gu")
        res = (x, (tm, 512), lambda jj, i: (i, jj))
        (x,) = _matmul(hid, w_down, e, [0], d, tm, 512, _epi_gated_residual(e), [res, gate_x], [F32],
                       "moe_down")
    return x, s_fin, proj


def kernel(x_prompt, x_sample, cache_k, cache_v, state_delta, state_conv, page_table, norm_mix, norm_ffn, norm_final, w_qkv_da, lambda_q1, lambda_k1, lambda_q2, lambda_k2, subln_da, w_o_da, w_in_gd, conv_w_gd, a_log_gd, dt_bias_gd, gnorm_gd, w_o_gd, w_gu_dense, w_down_dense, w_router, w_gu_moe, w_down_moe):
    w = dict(norm_mix=norm_mix, norm_ffn=norm_ffn, w_qkv_da=w_qkv_da, lambda_q1=lambda_q1,
             lambda_k1=lambda_k1, lambda_q2=lambda_q2, lambda_k2=lambda_k2, subln_da=subln_da,
             w_o_da=w_o_da, w_in_gd=w_in_gd, conv_w_gd=conv_w_gd, a_log_gd=a_log_gd,
             dt_bias_gd=dt_bias_gd, gnorm_gd=gnorm_gd, w_o_gd=w_o_gd, w_gu_dense=w_gu_dense,
             w_down_dense=w_down_dense, w_router=w_router, w_gu_moe=w_gu_moe, w_down_moe=w_down_moe)
    b, s, d = x_prompt.shape
    bd, t_dec, _ = x_sample.shape
    assert t_dec == 1
    n_pages = page_table.shape[1]
    past = n_pages * PAGE_SIZE
    tm_p = 512
    tm_s = bd * t_dec
    depth = norm_mix.shape[0]

    xp = x_prompt.reshape(b * s, d)
    xs = x_sample.reshape(bd * t_dec, d)
    cos_p, sin_p = _rope_tables(s, 0, s)
    cos_s, sin_s = _rope_tables(tm_s, past, t_dec)
    rope_p = (cos_p, sin_p, s // tm_p)
    rope_s = (cos_s, sin_s, 1)

    kp_rows, vp_rows, ks_rows, vs_rows = [], [], [], []
    sp_fin, cp_fin, ss_fin, cs_fin = [], [], [], []
    for i in range(depth):
        j = i // 2
        if i % 2 == 0:
            def attend_prompt(q, k, v, lam_vecs, subln, lam_init):
                return _attn_prompt(q, k, v, lam_vecs, subln, b, s, lam_init)

            def attend_sample(q, k, v, lam_vecs, subln, lam_init, j=j):
                q4 = q.reshape(bd, DA_HEADS, 2, DA_HEAD_DIM).transpose(0, 2, 1, 3)
                k_new = jnp.pad(k.reshape(bd, 1, DA_SUB_HEADS, DA_HEAD_DIM),
                                ((0, 0), (0, SUBLANES - 1), (0, 0), (0, 0)))
                v_new = jnp.pad(v.reshape(bd, 1, DA_HEADS, DA_V_DIM),
                                ((0, 0), (0, SUBLANES - 1), (0, 0), (0, 0)))
                o = _attn_decode(q4, cache_k, cache_v, j, page_table, k_new, v_new, lam_vecs, subln,
                                 lam_init)
                return o.reshape(bd, DA_HEADS * DA_V_DIM)

            xp, kp, vp = _attention_layer(xp, tm_p, i, j, w, rope_p, attend_prompt, False)
            xs, ks, vs = _attention_layer(xs, tm_s, i, j, w, rope_s, attend_sample, True)
            kp_rows.append(kp.reshape(b, s, DA_SUB_HEADS, DA_HEAD_DIM))
            vp_rows.append(vp.reshape(b, s, DA_HEADS, DA_V_DIM))
            ks_rows.append(ks.reshape(bd, t_dec, DA_SUB_HEADS, DA_HEAD_DIM))
            vs_rows.append(vs.reshape(bd, t_dec, DA_HEADS, DA_V_DIM))
        else:
            zero_buf = jnp.zeros((b * SUBLANES, GD_CONV_CH), F32)
            zero_state = jnp.zeros((b, GD_V_HEADS, GD_HEAD_DIM, GD_HEAD_DIM), F32)
            xp, sp, proj_p = _deltanet_layer(xp, tm_p, i, j, w, b, s, zero_buf, zero_state, False)
            buf_s = jnp.pad(state_conv[j], ((0, 0), (SUBLANES - (CONV_WIDTH - 1), 0), (0, 0)))
            xs, ss, proj_s = _deltanet_layer(xs, tm_s, i, j, w, bd, t_dec,
                                             buf_s.reshape(bd * SUBLANES, GD_CONV_CH), state_delta[j], True)
            sp_fin.append(sp)
            ss_fin.append(ss)
            cp_fin.append(proj_p.reshape(b, s, -1)[:, s - (CONV_WIDTH - 1):, :GD_CONV_CH])
            cs_all = jnp.concatenate([state_conv[j], proj_s.reshape(bd, t_dec, -1)[:, :, :GD_CONV_CH]], axis=1)
            cs_fin.append(cs_all[:, t_dec:])
    y_prompt = _rmsnorm(xp, norm_final, F32, tm_p).reshape(b, s, d)
    y_sample = _rmsnorm(xs, norm_final, F32, tm_s).reshape(bd, t_dec, d)
    return (y_prompt, y_sample,
            jnp.stack(kp_rows), jnp.stack(vp_rows), jnp.stack(ks_rows), jnp.stack(vs_rows),
            jnp.stack(sp_fin), jnp.stack(cp_fin), jnp.stack(ss_fin), jnp.stack(cs_fin))
```

```python
import functools
import math

import numpy as np
import jax
import jax.numpy as jnp
from jax import lax
from jax.experimental import pallas as pl
from jax.experimental.pallas import tpu as pltpu

F32 = jnp.float32
BF16 = jnp.bfloat16

PAGE_SIZE = 128
DA_HEAD_DIM = 64
DA_V_DIM = 128
DA_HEADS = 16
DA_SUB_HEADS = 32
DA_QK_WIDTH = 2048
ROPE_THETA = 500000.0
ROT_DIM = 16
LAMBDA_INIT_BASE = 0.8
LAMBDA_INIT_AMP = 0.6
LAMBDA_INIT_RATE = 0.3
GD_K_HEADS = 16
GD_V_HEADS = 32
GD_HEAD_DIM = 128
GD_KEY_WIDTH = 2048
GD_VAL_WIDTH = 4096
GD_CONV_CH = 8192
CONV_WIDTH = 4
CHUNK = 64
N_EXPERTS = 8
NORM_EPS = 1e-6
L2_EPS = 1e-6

LANES = 128
SUBLANES = 8
VMEM_LIMIT_BYTES = 56 * 1024 * 1024


def _params(n_grid_dims, vmem=VMEM_LIMIT_BYTES):
    return pltpu.CompilerParams(dimension_semantics=("arbitrary",) * n_grid_dims,
                                vmem_limit_bytes=vmem)


def _sigmoid(x):
    return 1.0 / (1.0 + jnp.exp(-x))


def _silu(x):
    return x * _sigmoid(x)


def _dot(a, b):
    return jnp.dot(a, b, preferred_element_type=F32)


def _dot_nt(a, b):
    return lax.dot_general(a, b, (((1,), (1,)), ((), ())), preferred_element_type=F32)


def _dot_tn(a, b):
    return lax.dot_general(a, b, (((0,), (0,)), ((), ())), preferred_element_type=F32)


def _dot_f32(a, b):
    return jnp.dot(a, b, preferred_element_type=F32, precision=lax.Precision.HIGHEST)


def _split_bf16(x):
    hi = x.astype(BF16)
    return hi, (x - hi.astype(F32)).astype(BF16)


def _rmsnorm_kernel(x_ref, w_ref, o_ref):
    x = x_ref[...]
    ms = jnp.mean(x * x, axis=-1, keepdims=True)
    o_ref[...] = (x * lax.rsqrt(ms + NORM_EPS) * w_ref[...]).astype(o_ref.dtype)


def _rmsnorm(x, w, out_dtype, tm):
    m, d = x.shape
    return pl.pallas_call(
        _rmsnorm_kernel,
        grid=(m // tm,),
        in_specs=[pl.BlockSpec((tm, d), lambda i: (i, 0)),
                  pl.BlockSpec((1, d), lambda i: (0, 0))],
        out_specs=pl.BlockSpec((tm, d), lambda i: (i, 0)),
        out_shape=jax.ShapeDtypeStruct((m, d), out_dtype),
        compiler_params=_params(1),
        name="rmsnorm",
    )(x, w.reshape(1, d))


def _mm_kernel(*refs, n_w, n_x, n_o, epilogue, precise):
    a_ref = refs[0]
    w_refs = refs[1:1 + n_w]
    x_refs = refs[1 + n_w:1 + n_w + n_x]
    o_refs = refs[1 + n_w + n_x:1 + n_w + n_x + n_o]
    wbf_refs = refs[1 + n_w + n_x + n_o:]

    a = a_ref[...]
    if precise:
        accs = [_dot_f32(a, w_ref[...]) for w_ref in w_refs]
    else:
        @pl.when(pl.program_id(1) == 0)
        def _():
            for w_ref, wbf_ref in zip(w_refs, wbf_refs):
                wbf_ref[...] = w_ref[...].astype(BF16)

        accs = [_dot(a, wbf_ref[...]) for wbf_ref in wbf_refs]
    for o_ref, val in zip(o_refs, epilogue(accs, x_refs)):
        o_ref[...] = val.astype(o_ref.dtype)


def _matmul(a, w, widx, col_starts, n_cols, tm, tn, epilogue, extras, out_dtypes, name):
    m, k = a.shape
    precise = a.dtype == F32
    in_specs = [pl.BlockSpec((tm, k), lambda j, i: (i, 0))]
    for cs in col_starts:
        in_specs.append(pl.BlockSpec((None, k, tn), lambda j, i, cb=cs // tn: (widx, 0, cb + j)))
    for _, bs, im in extras:
        in_specs.append(pl.BlockSpec(bs, im))
    kernel = functools.partial(_mm_kernel, n_w=len(col_starts), n_x=len(extras),
                               n_o=len(out_dtypes), epilogue=epilogue, precise=precise)
    outs = pl.pallas_call(
        kernel,
        grid=(n_cols // tn, m // tm),
        in_specs=in_specs,
        out_specs=[pl.BlockSpec((tm, tn), lambda j, i: (i, j)) for _ in out_dtypes],
        out_shape=[jax.ShapeDtypeStruct((m, n_cols), dt) for dt in out_dtypes],
        scratch_shapes=[] if precise else [pltpu.VMEM((k, tn), BF16) for _ in col_starts],
        compiler_params=_params(2),
        name=name,
    )(a, *([w] * len(col_starts)), *[e[0] for e in extras])
    return outs


def _epi_plain(n_out):
    return lambda accs, xs: [accs[0]] * n_out


def _epi_residual(accs, xs):
    return [xs[0][...] + accs[0]]


def _epi_swiglu(accs, xs):
    return [_silu(accs[0]) * accs[1]]


def _epi_gated_residual(expert):
    def epi(accs, xs):
        gate = xs[1][...][:, expert:expert + 1]
        return [xs[0][...] + gate * accs[0]]
    return epi


def _apply_rope(acc, cos_ref, sin_ref):
    cos = cos_ref[...]
    sin = sin_ref[...]
    lane = lax.broadcasted_iota(jnp.int32, (1, LANES), 1) % DA_HEAD_DIM
    first_half = lane < ROT_DIM // 2
    cols = []
    for c in range(acc.shape[1] // LANES):
        x = acc[:, c * LANES:(c + 1) * LANES]
        partner = jnp.where(first_half, pltpu.roll(x, LANES - ROT_DIM // 2, 1),
                            pltpu.roll(x, ROT_DIM // 2, 1))
        cols.append(x * cos + partner * sin)
    return jnp.concatenate(cols, axis=1)


def _epi_rope(scales):
    def epi(accs, xs):
        r = _apply_rope(accs[0], xs[0], xs[1])
        return [r if s == 1.0 else r * s for s in scales]
    return epi


def _rope_table_kernel(inv_ref, sgn_ref, cos_ref, sin_ref, *, offset, period):
    rows = cos_ref.shape[0]
    t = lax.broadcasted_iota(jnp.int32, (rows, LANES), 0) % period
    pos = (t + offset).astype(F32)
    ang = pos * inv_ref[...]
    sgn = sgn_ref[...]
    cos_ref[...] = jnp.where(sgn != 0.0, jnp.cos(ang), 1.0)
    sin_ref[...] = sgn * jnp.sin(ang)


def _rope_tables(rows, offset, period):
    half = ROT_DIM // 2
    inv = ROPE_THETA ** (-2.0 * np.arange(half, dtype=np.float32) / ROT_DIM)
    lane = np.arange(LANES) % DA_HEAD_DIM
    inv_lane = np.where(lane < ROT_DIM, inv[lane % half], 0.0).astype(np.float32)
    sgn_lane = np.where(lane < half, -1.0, np.where(lane < ROT_DIM, 1.0, 0.0)).astype(np.float32)
    kernel = functools.partial(_rope_table_kernel, offset=offset, period=period)
    return pl.pallas_call(
        kernel,
        out_shape=[jax.ShapeDtypeStruct((rows, LANES), F32)] * 2,
        name="rope_tables",
    )(jnp.asarray(inv_lane).reshape(1, LANES), jnp.asarray(sgn_lane).reshape(1, LANES))


def _diff_lambda(lam_ref, lam_init):
    lv = lam_ref[...]
    a = jnp.sum(lv[0:1] * lv[1:2], axis=-1, keepdims=True)
    b = jnp.sum(lv[2:3] * lv[3:4], axis=-1, keepdims=True)
    return jnp.exp(a) - jnp.exp(b) + lam_init


def _sub_layer_norm(o, subln_ref, lam_init):
    ms = jnp.mean(o * o, axis=-1, keepdims=True)
    return o * lax.rsqrt(ms + NORM_EPS) * subln_ref[...] * (1.0 - lam_init)


def _attn_prompt_kernel(lam_ref, subln_ref, q_ref, k_ref, v_ref, o_ref, m_sc, l_sc, acc_sc,
                        *, tq, tk, lam_init):
    qi = pl.program_id(2)
    ki = pl.program_id(3)

    @pl.when(ki == 0)
    def _():
        m_sc[...] = jnp.full(m_sc.shape, -jnp.inf, F32)
        l_sc[...] = jnp.zeros(l_sc.shape, F32)
        acc_sc[...] = jnp.zeros(acc_sc.shape, F32)

    @pl.when(ki <= qi)
    def _():
        q = q_ref[...]
        k = k_ref[...]
        v = v_ref[...]
        lane = lax.broadcasted_iota(jnp.int32, (1, LANES), 1)
        row = lax.broadcasted_iota(jnp.int32, (tq, tk), 0) + qi * tq
        col = lax.broadcasted_iota(jnp.int32, (tq, tk), 1) + ki * tk
        causal = col <= row
        zero = jnp.zeros_like(q)
        for sub in range(2):
            in_sub = (lane < DA_HEAD_DIM) if sub == 0 else (lane >= DA_HEAD_DIM)
            s = _dot_nt(jnp.where(in_sub, q, zero), k)
            s = jnp.where(causal, s, -jnp.inf)
            m_prev = m_sc[sub]
            m_new = jnp.maximum(m_prev, jnp.max(s, axis=-1, keepdims=True))
            alpha = jnp.exp(m_prev - m_new)
            p = jnp.exp(s - m_new[:, :1])
            l_sc[sub] = alpha * l_sc[sub] + jnp.sum(p, axis=-1, keepdims=True)
            acc_sc[sub] = alpha * acc_sc[sub] + _dot(p.astype(BF16), v)
            m_sc[sub] = m_new

    @pl.when(ki == qi)
    def _():
        lam = _diff_lambda(lam_ref, lam_init)
        o = acc_sc[0] / l_sc[0] - lam * (acc_sc[1] / l_sc[1])
        o_ref[...] = _sub_layer_norm(o, subln_ref, lam_init).astype(o_ref.dtype)


def _attn_prompt(q, k, v, lam_vecs, subln, batch, seq, lam_init, tq=512):
    nq = seq // tq
    kernel = functools.partial(_attn_prompt_kernel, tq=tq, tk=tq, lam_init=lam_init)
    kv_spec = pl.BlockSpec((tq, LANES), lambda b, h, qi, ki: (b * nq + jnp.minimum(ki, qi), h))
    return pl.pallas_call(
        kernel,
        grid=(batch, DA_HEADS, nq, nq),
        in_specs=[pl.BlockSpec((SUBLANES, LANES), lambda b, h, qi, ki: (0, 0)),
                  pl.BlockSpec((1, LANES), lambda b, h, qi, ki: (0, 0)),
                  pl.BlockSpec((tq, LANES), lambda b, h, qi, ki: (b * nq + qi, h)),
                  kv_spec, kv_spec],
        out_specs=pl.BlockSpec((tq, LANES), lambda b, h, qi, ki: (b * nq + qi, h)),
        out_shape=jax.ShapeDtypeStruct((batch * seq, DA_HEADS * DA_V_DIM), BF16),
        scratch_shapes=[pltpu.VMEM((2, tq, LANES), F32), pltpu.VMEM((2, tq, LANES), F32),
                        pltpu.VMEM((2, tq, LANES), F32)],
        compiler_params=_params(4),
        name="attn_prompt",
    )(lam_vecs, subln, q, k, v)


def _attn_decode_kernel(pt_ref, lam_ref, subln_ref, spread_ref, qrep_ref, q_ref, kn_ref, vn_ref, *rest,
                        lam_init, n_blk):
    del pt_ref
    k_refs = rest[:n_blk]
    v_refs = rest[n_blk:2 * n_blk]
    o_ref, m_sc, l_sc, acc_sc = rest[2 * n_blk:]
    p = pl.program_id(1)

    @pl.when(p == 0)
    def _():
        m_sc[...] = jnp.full(m_sc.shape, -jnp.inf, F32)
        l_sc[...] = jnp.zeros(l_sc.shape, F32)
        acc_sc[...] = jnp.zeros(acc_sc.shape, F32)

    def update(s, weighted_values):
        m_prev = m_sc[...]
        m_new = jnp.maximum(m_prev, jnp.max(s, axis=-1, keepdims=True))
        alpha = jnp.exp(m_prev - m_new)
        pe = jnp.exp(s - m_new[:, :1])
        l_sc[...] = alpha * l_sc[...] + jnp.sum(pe, axis=-1, keepdims=True)
        acc_sc[...] = alpha * acc_sc[...] + weighted_values(pe)
        m_sc[...] = m_new

    width = PAGE_SIZE * DA_HEADS
    row_head = (lax.broadcasted_iota(jnp.int32, (2 * DA_SUB_HEADS, width), 0) % DA_SUB_HEADS) // 2
    col_head = lax.broadcasted_iota(jnp.int32, (2 * DA_SUB_HEADS, width), 1) % DA_HEADS
    own_head = row_head == col_head
    q_rep = qrep_ref[...]
    for k_ref, v_ref in zip(k_refs, v_refs):
        s = jnp.sum(k_ref[...] * q_rep, axis=1)

        def page_values(pe, v_ref=v_ref):
            p_hi, p_lo = _split_bf16(pe)
            spread = _dot(jnp.concatenate([p_hi, p_lo], axis=0), spread_ref[...])
            pm = jnp.where(own_head, spread, 0.0).astype(BF16)
            v_hi, v_lo = _split_bf16(v_ref[...].reshape(width, DA_V_DIM))
            both = _dot(pm, v_hi)
            return both[:DA_SUB_HEADS] + both[DA_SUB_HEADS:] + _dot(pm[:DA_SUB_HEADS], v_lo)

        update(s, page_values)

    @pl.when(p == pl.num_programs(1) - 1)
    def _():
        s_new = jnp.sum(q_ref[...] * kn_ref[...], axis=-1, keepdims=True)
        update(s_new, lambda pe: pe * vn_ref[...])
        lam = _diff_lambda(lam_ref, lam_init)
        acc_sc[...] = acc_sc[...] / l_sc[...]
        o = (acc_sc[pl.ds(0, DA_HEADS, stride=2), :]
             - lam * acc_sc[pl.ds(1, DA_HEADS, stride=2), :])
        o_ref[...] = _sub_layer_norm(o, subln_ref, lam_init).astype(o_ref.dtype)


def _attn_decode(q, cache_k, cache_v, layer, page_table, k_new, v_new, lam_vecs, subln, lam_init,
                 pages_per_step=4):
    bd, n_pages = page_table.shape
    n_blk = pages_per_step
    assert n_pages % n_blk == 0
    keys_t = jnp.transpose(cache_k, (0, 1, 3, 4, 2))
    q_rep = jnp.broadcast_to(q[..., None], q.shape + (PAGE_SIZE,))
    v_new2 = jnp.repeat(v_new, 2, axis=1)
    width = PAGE_SIZE * DA_HEADS
    spread = jnp.asarray(np.arange(width)[None, :] // DA_HEADS == np.arange(PAGE_SIZE)[:, None], BF16)
    kernel = functools.partial(_attn_decode_kernel, lam_init=lam_init, n_blk=n_blk)
    per_seq = lambda *dims: pl.BlockSpec((None,) + dims, lambda b, p, pt: (b,) + (0,) * len(dims))
    page = lambda dims, i: pl.BlockSpec(
        (None, None) + dims, lambda b, p, pt: (layer, pt[b, p * n_blk + i], 0, 0, 0))
    grid_spec = pltpu.PrefetchScalarGridSpec(
        num_scalar_prefetch=1,
        grid=(bd, n_pages // n_blk),
        in_specs=[
            pl.BlockSpec((SUBLANES, LANES), lambda b, p, pt: (0, 0)),
            pl.BlockSpec((1, LANES), lambda b, p, pt: (0, 0)),
            pl.BlockSpec((PAGE_SIZE, width), lambda b, p, pt: (0, 0)),
            per_seq(DA_SUB_HEADS, DA_HEAD_DIM, PAGE_SIZE),
            per_seq(DA_SUB_HEADS, DA_HEAD_DIM),
            per_seq(DA_SUB_HEADS, DA_HEAD_DIM),
            per_seq(DA_SUB_HEADS, DA_V_DIM),
        ] + [page((DA_SUB_HEADS, DA_HEAD_DIM, PAGE_SIZE), i) for i in range(n_blk)]
          + [page((PAGE_SIZE, DA_HEADS, DA_V_DIM), i) for i in range(n_blk)],
        out_specs=per_seq(DA_HEADS, DA_V_DIM),
        scratch_shapes=[pltpu.VMEM((DA_SUB_HEADS, LANES), F32), pltpu.VMEM((DA_SUB_HEADS, LANES), F32),
                        pltpu.VMEM((DA_SUB_HEADS, DA_V_DIM), F32)],
    )
    return pl.pallas_call(
        kernel,
        grid_spec=grid_spec,
        out_shape=jax.ShapeDtypeStruct((bd, DA_HEADS, DA_V_DIM), F32),
        compiler_params=_params(2),
        name="attn_decode",
    )(page_table, lam_vecs, subln, spread, q_rep, q, k_new, v_new2,
      *([keys_t] * n_blk), *([cache_v] * n_blk))


def _gdn_gates_kernel(ba_ref, alog_ref, dtb_ref, o_ref):
    x = ba_ref[...]
    lane = lax.broadcasted_iota(jnp.int32, x.shape, 1)
    beta = _sigmoid(x)
    y = x + dtb_ref[...]
    softplus = jnp.maximum(y, 0.0) + jnp.log1p(jnp.exp(-jnp.abs(y)))
    g = -jnp.exp(alog_ref[...]) * softplus
    o_ref[...] = jnp.where(lane < GD_V_HEADS, beta, jnp.where(lane < 2 * GD_V_HEADS, g, 0.0))


def _gdn_gates(ba, a_log, dt_bias, tm):
    m = ba.shape[0]
    pad = lambda v: jnp.pad(v.reshape(1, GD_V_HEADS), ((0, 0), (GD_V_HEADS, LANES - 2 * GD_V_HEADS)))
    return pl.pallas_call(
        _gdn_gates_kernel,
        grid=(m // tm,),
        in_specs=[pl.BlockSpec((tm, LANES), lambda i: (i, 0)),
                  pl.BlockSpec((1, LANES), lambda i: (0, 0)),
                  pl.BlockSpec((1, LANES), lambda i: (0, 0))],
        out_specs=pl.BlockSpec((tm, LANES), lambda i: (i, 0)),
        out_shape=jax.ShapeDtypeStruct((m, LANES), F32),
        compiler_params=_params(1),
        name="gdn_gates",
    )(ba, pad(a_log), pad(dt_bias))


def _gdn_conv_kernel(x_ref, buf_ref, w_ref, o_ref, xin_sc, *, heads):
    t = x_ref.shape[0]
    xin_sc[0:SUBLANES, :] = buf_ref[...]
    xin_sc[SUBLANES:SUBLANES + t, :] = x_ref[...]
    for hh in range(heads):
        c = pl.program_id(1) * heads + hh
        cols = slice(hh * LANES, (hh + 1) * LANES)
        w = w_ref[:, cols]
        y = x_ref[:, cols] * w[CONV_WIDTH - 1:CONV_WIDTH]
        for s in range(1, CONV_WIDTH):
            shifted = xin_sc[pl.ds(SUBLANES - s, t), cols]
            y = y + shifted * w[CONV_WIDTH - 1 - s:CONV_WIDTH - s]
        y = _silu(y)
        inv = lax.rsqrt(jnp.sum(y * y, axis=-1, keepdims=True) + L2_EPS)
        scale = jnp.where(c < GD_K_HEADS, GD_HEAD_DIM ** -0.5, 1.0)
        o_ref[:, cols] = jnp.where(c < 2 * GD_K_HEADS, y * (inv * scale), y)


def _gdn_conv(x, buf, conv_w, widx, batch, t, heads):
    width = heads * LANES
    return pl.pallas_call(
        functools.partial(_gdn_conv_kernel, heads=heads),
        grid=(batch, GD_CONV_CH // width),
        in_specs=[pl.BlockSpec((t, width), lambda b, c: (b, c)),
                  pl.BlockSpec((SUBLANES, width), lambda b, c: (b, c)),
                  pl.BlockSpec((None, CONV_WIDTH, width), lambda b, c: (widx, 0, c))],
        out_specs=pl.BlockSpec((t, width), lambda b, c: (b, c)),
        out_shape=jax.ShapeDtypeStruct((batch * t, GD_CONV_CH), F32),
        scratch_shapes=[pltpu.VMEM((SUBLANES + t, width), F32)],
        compiler_params=_params(2),
        name="gdn_conv",
    )(x, buf, conv_w)


def _split_products(lhs_list, rhs_list):
    ls = [_split_bf16(a) for a in lhs_list]
    rs = [_split_bf16(b) for b in rhs_list]
    first = [_dot(jnp.concatenate([a_hi, a_lo], axis=0), b_hi) for (a_hi, a_lo), (b_hi, _) in zip(ls, rs)]
    second = [_dot(a_hi, b_lo) for (a_hi, _), (_, b_lo) in zip(ls, rs)]
    outs = []
    for f, s, a in zip(first, second, lhs_list):
        n = a.shape[0]
        outs.append(f[:n] + f[n:] + s)
    return outs


def _unit_lower_inverses(ms, eye):
    n = eye.shape[0]
    invs = [eye - m for m in ms]
    powers = _split_products(ms, ms)
    n_factors = int(math.log2(CHUNK)) - 1
    for f in range(n_factors):
        last = f + 1 == n_factors
        lhs = invs if last else [jnp.concatenate([i, p], axis=0) for i, p in zip(invs, powers)]
        prods = _split_products(lhs, powers)
        invs = [i + pr[:n] for i, pr in zip(invs, prods)]
        if not last:
            powers = [pr[n:] for pr in prods]
    return invs


def _gdn_chunk_kernel(q_ref, k_ref, v_ref, z_ref, bg_ref, gt_ref, gn_ref, s0_ref, o_ref, sfin_ref,
                      s_sc, *, heads):
    grp = pl.program_id(1)
    c = pl.program_id(2)
    pairs = heads // 2
    two = 2 * CHUNK

    @pl.when(c == 0)
    def _():
        s_sc[...] = s0_ref[...].reshape(s_sc.shape)

    bg = bg_ref[...]
    gn = gn_ref[...]
    lane = lax.broadcasted_iota(jnp.int32, (1, LANES), 1)
    ri = lax.broadcasted_iota(jnp.int32, (two, two), 0)
    ci = lax.broadcasted_iota(jnp.int32, (two, two), 1)
    same = (ri // CHUNK) == (ci // CHUNK)
    tril = same & (ri >= ci)
    triu = same & (ri <= ci)
    strict = same & (ri > ci)
    eye = (ri == ci).astype(F32)
    top_rows = lax.broadcasted_iota(jnp.int32, (two, 1), 0) < CHUNK
    top_state = lax.broadcasted_iota(jnp.int32, (2 * GD_HEAD_DIM, 1), 0) < GD_HEAD_DIM

    def stack2(x):
        return jnp.concatenate([x, x], axis=0)

    def blocks(x):
        zero = jnp.zeros_like(x)
        return jnp.concatenate([jnp.where(top_rows, x, zero), jnp.where(top_rows, zero, x)], axis=1)

    def column(lane_idx):
        return jnp.sum(jnp.where(lane == lane_idx, bg, 0.0), axis=-1, keepdims=True)

    qs = [stack2(q_ref[:, kh * LANES:(kh + 1) * LANES]) for kh in range(pairs)]
    ks = [stack2(k_ref[:, kh * LANES:(kh + 1) * LANES]) for kh in range(pairs)]
    vs = [jnp.concatenate([v_ref[:, (2 * kh) * LANES:(2 * kh + 1) * LANES],
                           v_ref[:, (2 * kh + 1) * LANES:(2 * kh + 2) * LANES]], axis=0)
          for kh in range(pairs)]
    kq = [_dot_nt(jnp.concatenate([k, q], axis=0).astype(BF16), k.astype(BF16)) for k, q in zip(ks, qs)]

    betas, gcums, decays, egs = [], [], [], []
    for kh in range(pairs):
        h = grp * heads + 2 * kh
        beta = jnp.concatenate([column(h), column(h + 1)], axis=0)
        g_col = jnp.concatenate([column(h + GD_V_HEADS), column(h + 1 + GD_V_HEADS)], axis=0)
        g_row = gt_ref[kh:kh + 1, :]
        gcum_col = jnp.sum(jnp.where(tril, g_row, 0.0), axis=-1, keepdims=True)
        gcum_row = jnp.sum(jnp.where(triu, g_col, 0.0), axis=0, keepdims=True)
        betas.append(beta)
        gcums.append(gcum_col)
        decays.append(jnp.exp(jnp.where(tril, gcum_col - gcum_row, -jnp.inf)))
        egs.append(jnp.exp(gcum_col))

    tinvs = _unit_lower_inverses(
        [jnp.where(strict, b * x[:two] * d, 0.0) for b, x, d in zip(betas, kq, decays)], eye)
    rhs = [jnp.concatenate([v * b, k * (b * e)], axis=1).astype(BF16)
           for v, k, b, e in zip(vs, ks, betas, egs)]
    wus = [_dot(t.astype(BF16), r) for t, r in zip(tinvs, rhs)]
    s_olds = [s_sc[kh] for kh in range(pairs)]
    s_bfs = [s.astype(BF16) for s in s_olds]
    v_news = [wu[:, :LANES] - _dot(blocks(wu[:, LANES:]).astype(BF16), s)
              for wu, s in zip(wus, s_bfs)]
    v_bfs = [v.astype(BF16) for v in v_news]
    outs = [_dot(jnp.concatenate([blocks(q * e), x[two:] * d], axis=1).astype(BF16),
                 jnp.concatenate([s, v], axis=0))
            for q, e, x, d, s, v in zip(qs, egs, kq, decays, s_bfs, v_bfs)]
    g_lasts = [jnp.where(top_rows, g[CHUNK - 1:CHUNK, :], g[two - 1:two, :]) for g in gcums]
    updates = [_dot_tn(blocks(k * jnp.exp(gl - g)).astype(BF16), v)
               for k, gl, g, v in zip(ks, g_lasts, gcums, v_bfs)]
    for kh in range(pairs):
        g = gcums[kh]
        carry = jnp.exp(jnp.where(top_state, g[CHUNK - 1:CHUNK, :], g[two - 1:two, :]))
        s_sc[kh] = s_olds[kh] * carry + updates[kh]
        o = outs[kh]
        z = jnp.concatenate([z_ref[:, (2 * kh) * LANES:(2 * kh + 1) * LANES],
                             z_ref[:, (2 * kh + 1) * LANES:(2 * kh + 2) * LANES]], axis=0)
        ms = jnp.mean(o * o, axis=-1, keepdims=True)
        on = (o * lax.rsqrt(ms + NORM_EPS) * gn * _silu(z)).astype(o_ref.dtype)
        o_ref[:, (2 * kh) * LANES:(2 * kh + 1) * LANES] = on[:CHUNK]
        o_ref[:, (2 * kh + 1) * LANES:(2 * kh + 2) * LANES] = on[CHUNK:]

    @pl.when(c == pl.num_programs(2) - 1)
    def _():
        sfin_ref[...] = s_sc[...].reshape(sfin_ref.shape)


def _gdn_chunks(qkv, proj, bg, g_t, gnorm, s0, batch, t, heads=16):
    nc = t // CHUNK
    n_grp = GD_V_HEADS // heads
    pairs = heads // 2
    kw = pairs * LANES
    vw = heads * LANES
    row = lambda b, g, c: b * nc + c
    kernel = functools.partial(_gdn_chunk_kernel, heads=heads)
    return pl.pallas_call(
        kernel,
        grid=(batch, n_grp, nc),
        in_specs=[
            pl.BlockSpec((CHUNK, kw), lambda b, g, c: (row(b, g, c), g)),
            pl.BlockSpec((CHUNK, kw), lambda b, g, c: (row(b, g, c), GD_KEY_WIDTH // kw + g)),
            pl.BlockSpec((CHUNK, vw), lambda b, g, c: (row(b, g, c), 2 * GD_KEY_WIDTH // vw + g)),
            pl.BlockSpec((CHUNK, vw), lambda b, g, c: (row(b, g, c), GD_CONV_CH // vw + g)),
            pl.BlockSpec((CHUNK, LANES), lambda b, g, c: (row(b, g, c), 0)),
            pl.BlockSpec((None, None, pairs, 2 * CHUNK), lambda b, g, c: (b, c, g, 0)),
            pl.BlockSpec((1, LANES), lambda b, g, c: (0, 0)),
            pl.BlockSpec((None, heads, GD_HEAD_DIM, GD_HEAD_DIM), lambda b, g, c: (b, g, 0, 0)),
        ],
        out_specs=[
            pl.BlockSpec((CHUNK, vw), lambda b, g, c: (row(b, g, c), g)),
            pl.BlockSpec((None, heads, GD_HEAD_DIM, GD_HEAD_DIM), lambda b, g, c: (b, g, 0, 0)),
        ],
        out_shape=[jax.ShapeDtypeStruct((batch * t, GD_VAL_WIDTH), BF16),
                   jax.ShapeDtypeStruct((batch, GD_V_HEADS, GD_HEAD_DIM, GD_HEAD_DIM), F32)],
        scratch_shapes=[pltpu.VMEM((pairs, 2 * GD_HEAD_DIM, GD_HEAD_DIM), F32)],
        compiler_params=_params(3),
        name="gdn_chunks",
    )(qkv, qkv, qkv, proj, bg, g_t, gnorm, s0)


def _gdn_step_kernel(qkv_ref, z_ref, bg_ref, gn_ref, s0_ref, o_ref, sfin_ref):
    ri = lax.broadcasted_iota(jnp.int32, (GD_HEAD_DIM, GD_HEAD_DIM), 0)
    ci = lax.broadcasted_iota(jnp.int32, (GD_HEAD_DIM, GD_HEAD_DIM), 1)
    diag = ri == ci
    bg = bg_ref[...]
    gn = gn_ref[...]

    def column(row_vec):
        return jnp.sum(jnp.where(diag, row_vec, 0.0), axis=-1, keepdims=True)

    for kh in range(GD_K_HEADS):
        q_col = column(qkv_ref[:, kh * LANES:(kh + 1) * LANES])
        k_col = column(qkv_ref[:, GD_KEY_WIDTH + kh * LANES:GD_KEY_WIDTH + (kh + 1) * LANES])
        for hh in (2 * kh, 2 * kh + 1):
            v = qkv_ref[:, 2 * GD_KEY_WIDTH + hh * LANES:2 * GD_KEY_WIDTH + (hh + 1) * LANES]
            beta = bg[:, hh:hh + 1]
            eg = jnp.exp(bg[:, GD_V_HEADS + hh:GD_V_HEADS + hh + 1])
            s_old = s0_ref[hh]
            sk = jnp.sum(s_old * k_col, axis=0, keepdims=True)
            v_new = beta * (v - eg * sk)
            s_new = eg * s_old + k_col * v_new
            sfin_ref[hh] = s_new
            o = jnp.sum(s_new * q_col, axis=0, keepdims=True)
            z = z_ref[:, hh * LANES:(hh + 1) * LANES]
            ms = jnp.mean(o * o, axis=-1, keepdims=True)
            o_ref[:, hh * LANES:(hh + 1) * LANES] = o * lax.rsqrt(ms + NORM_EPS) * gn * _silu(z)


def _gdn_step(qkv, z, bg, gnorm, s0):
    batch = qkv.shape[0]
    row = lambda width: pl.BlockSpec((None, 1, width), lambda b: (b, 0, 0))
    state = pl.BlockSpec((None, GD_V_HEADS, GD_HEAD_DIM, GD_HEAD_DIM), lambda b: (b, 0, 0, 0))
    return pl.pallas_call(
        _gdn_step_kernel,
        grid=(batch,),
        in_specs=[row(GD_CONV_CH), row(GD_VAL_WIDTH), row(LANES),
                  pl.BlockSpec((1, LANES), lambda b: (0, 0)), state],
        out_specs=[row(GD_VAL_WIDTH), state],
        out_shape=[jax.ShapeDtypeStruct((batch, 1, GD_VAL_WIDTH), F32),
                   jax.ShapeDtypeStruct(s0.shape, F32)],
        compiler_params=_params(1),
        name="gdn_step",
    )(qkv, z, bg, gnorm, s0)


def _router_kernel(h_ref, w_ref, o_ref):
    if h_ref.dtype == F32:
        logits = _dot_f32(h_ref[...], w_ref[...])
    else:
        logits = _dot(h_ref[...], w_ref[...].astype(BF16))
    e = jnp.exp(logits - jnp.max(logits, axis=-1, keepdims=True))
    probs = e / jnp.sum(e, axis=-1, keepdims=True)
    lane = lax.broadcasted_iota(jnp.int32, probs.shape, 1)
    v1 = jnp.max(probs, axis=-1, keepdims=True)
    i1 = jnp.min(jnp.where(probs == v1, lane, N_EXPERTS), axis=-1, keepdims=True)
    rest = jnp.where(lane == i1, -1.0, probs)
    v2 = jnp.max(rest, axis=-1, keepdims=True)
    i2 = jnp.min(jnp.where(rest == v2, lane, N_EXPERTS), axis=-1, keepdims=True)
    total = v1 + v2
    o_ref[...] = jnp.where(lane == i1, v1 / total, 0.0) + jnp.where(lane == i2, v2 / total, 0.0)


def _router(h, w_router, widx, tm):
    m, d = h.shape
    return pl.pallas_call(
        _router_kernel,
        grid=(m // tm,),
        in_specs=[pl.BlockSpec((tm, d), lambda i: (i, 0)),
                  pl.BlockSpec((None, d, N_EXPERTS), lambda i: (widx, 0, 0))],
        out_specs=pl.BlockSpec((tm, N_EXPERTS), lambda i: (i, 0)),
        out_shape=jax.ShapeDtypeStruct((m, N_EXPERTS), F32),
        compiler_params=_params(1),
        name="moe_router",
    )(h, w_router)


def _lambda_vectors(lq1, lk1, lq2, lk2):
    rows = jnp.stack([lq1, lk1, lq2, lk2]).astype(F32)
    return jnp.pad(rows, ((0, SUBLANES - 4), (0, LANES - DA_HEAD_DIM)))


def _attention_layer(x, tm, layer, j, w, rope, attend, decode):
    d = x.shape[1]
    act = F32 if decode else BF16
    lam_init = LAMBDA_INIT_BASE - LAMBDA_INIT_AMP * math.exp(-LAMBDA_INIT_RATE * layer)
    lam_vecs = _lambda_vectors(w['lambda_q1'][j], w['lambda_k1'][j], w['lambda_q2'][j], w['lambda_k2'][j])
    subln = w['subln_da'][j].reshape(1, DA_V_DIM)
    cos, sin, n_tab = rope
    tabs = [(arr, (tm, LANES), lambda jj, i: (i % n_tab, 0)) for arr in (cos, sin)]
    h = _rmsnorm(x, w['norm_mix'][layer], act, tm)
    tn = 512
    kv_dtypes = [F32] if decode else [F32, BF16]
    (q,) = _matmul(h, w['w_qkv_da'], j, [0], DA_QK_WIDTH, tm, tn,
                   _epi_rope([DA_HEAD_DIM ** -0.5]), tabs, [act], "da_q")
    k = _matmul(h, w['w_qkv_da'], j, [DA_QK_WIDTH], DA_QK_WIDTH, tm, tn,
                _epi_rope([1.0] * len(kv_dtypes)), tabs, kv_dtypes, "da_k")
    v = _matmul(h, w['w_qkv_da'], j, [2 * DA_QK_WIDTH], DA_HEADS * DA_V_DIM, tm, tn,
                _epi_plain(len(kv_dtypes)), [], kv_dtypes, "da_v")
    o = attend(q, k[-1], v[-1], lam_vecs, subln, lam_init)
    res = (x, (tm, tn), lambda jj, i: (i, jj))
    (x,) = _matmul(o, w['w_o_da'], j, [0], d, tm, tn, _epi_residual, [res], [F32], "da_out")
    h = _rmsnorm(x, w['norm_ffn'][layer], act, tm)
    d_ff = w['w_down_dense'].shape[1]
    (hid,) = _matmul(h, w['w_gu_dense'], j, [0, d_ff], d_ff, tm, tn, _epi_swiglu, [], [act], "ffn_gu")
    tm_down = min(tm, 256)
    res = (x, (tm_down, tn), lambda jj, i: (i, jj))
    (x,) = _matmul(hid, w['w_down_dense'], j, [0], d, tm_down, tn, _epi_residual, [res], [F32], "ffn_down")
    return x, k[0], v[0]


def _deltanet_layer(x, tm, layer, j, w, batch, t, conv_buf, s0, decode):
    d = x.shape[1]
    act = F32 if decode else BF16
    h = _rmsnorm(x, w['norm_mix'][layer], act, tm)
    n_main = GD_CONV_CH + GD_VAL_WIDTH
    (proj,) = _matmul(h, w['w_in_gd'], j, [0], n_main, tm, 512, _epi_plain(1), [], [F32], "gd_in")
    (ba,) = _matmul(h, w['w_in_gd'], j, [n_main], LANES, tm, LANES, _epi_plain(1), [], [F32], "gd_in_ba")
    bg = _gdn_gates(ba, w['a_log_gd'][j], w['dt_bias_gd'][j], tm)
    gnorm = w['gnorm_gd'][j].reshape(1, GD_HEAD_DIM)
    if decode:
        conv_in = jnp.pad(proj.reshape(batch, 1, -1), ((0, 0), (0, SUBLANES - 1), (0, 0)))
        qkv = _gdn_conv(conv_in.reshape(batch * SUBLANES, -1), conv_buf, w['conv_w_gd'], j, batch,
                        SUBLANES, GD_K_HEADS)
        qkv = qkv.reshape(batch, SUBLANES, -1)[:, :1]
        z = proj[:, GD_CONV_CH:n_main].reshape(batch, 1, GD_VAL_WIDTH)
        o, s_fin = _gdn_step(qkv, z, bg.reshape(batch, 1, LANES), gnorm, s0)
        o = o.reshape(batch, GD_VAL_WIDTH)
    else:
        qkv = _gdn_conv(proj, conv_buf, w['conv_w_gd'], j, batch, t, 1)
        g_t = bg[:, GD_V_HEADS:2 * GD_V_HEADS].reshape(batch, t // CHUNK, CHUNK, GD_K_HEADS, 2)
        g_t = g_t.transpose(0, 1, 3, 4, 2).reshape(batch, t // CHUNK, GD_K_HEADS, 2 * CHUNK)
        o, s_fin = _gdn_chunks(qkv, proj, bg, g_t, gnorm, s0, batch, t)
    res = (x, (tm, 512), lambda jj, i: (i, jj))
    (x,) = _matmul(o, w['w_o_gd'], j, [0], d, tm, 512, _epi_residual, [res], [F32], "gd_out")
    h = _rmsnorm(x, w['norm_ffn'][layer], act, tm)
    gates = _router(h, w['w_router'], j, tm)
    h = h.astype(BF16)
    w_gu = w['w_gu_moe'][j]
    w_down = w['w_down_moe'][j]
    d_ffe = w_down.shape[1]
    gate_x = (gates, (tm, N_EXPERTS), lambda jj, i: (i, 0))
    for e in range(N_EXPERTS):
        (hid,) = _matmul(h, w_gu, e, [0, d_ffe], d_ffe, tm, 256, _epi_swiglu, [], [BF16], "moe_gu")
        res = (x, (tm, 512), lambda jj, i: (i, jj))
        (x,) = _matmul(hid, w_down, e, [0], d, tm, 512, _epi_gated_residual(e), [res, gate_x], [F32],
                       "moe_down")
    return x, s_fin, proj


def kernel(x_prompt, x_sample, cache_k, cache_v, state_delta, state_conv, page_table, norm_mix, norm_ffn, norm_final, w_qkv_da, lambda_q1, lambda_k1, lambda_q2, lambda_k2, subln_da, w_o_da, w_in_gd, conv_w_gd, a_log_gd, dt_bias_gd, gnorm_gd, w_o_gd, w_gu_dense, w_down_dense, w_router, w_gu_moe, w_down_moe):
    w = dict(norm_mix=norm_mix, norm_ffn=norm_ffn, w_qkv_da=w_qkv_da, lambda_q1=lambda_q1,
             lambda_k1=lambda_k1, lambda_q2=lambda_q2, lambda_k2=lambda_k2, subln_da=subln_da,
             w_o_da=w_o_da, w_in_gd=w_in_gd, conv_w_gd=conv_w_gd, a_log_gd=a_log_gd,
             dt_bias_gd=dt_bias_gd, gnorm_gd=gnorm_gd, w_o_gd=w_o_gd, w_gu_dense=w_gu_dense,
             w_down_dense=w_down_dense, w_router=w_router, w_gu_moe=w_gu_moe, w_down_moe=w_down_moe)
    b, s, d = x_prompt.shape
    bd, t_dec, _ = x_sample.shape
    assert t_dec == 1
    n_pages = page_table.shape[1]
    past = n_pages * PAGE_SIZE
    tm_p = 512
    tm_s = bd * t_dec
    depth = norm_mix.shape[0]

    xp = x_prompt.reshape(b * s, d)
    xs = x_sample.reshape(bd * t_dec, d)
    cos_p, sin_p = _rope_tables(s, 0, s)
    cos_s, sin_s = _rope_tables(tm_s, past, t_dec)
    rope_p = (cos_p, sin_p, s // tm_p)
    rope_s = (cos_s, sin_s, 1)

    kp_rows, vp_rows, ks_rows, vs_rows = [], [], [], []
    sp_fin, cp_fin, ss_fin, cs_fin = [], [], [], []
    for i in range(depth):
        j = i // 2
        if i % 2 == 0:
            def attend_prompt(q, k, v, lam_vecs, subln, lam_init):
                return _attn_prompt(q, k, v, lam_vecs, subln, b, s, lam_init)

            def attend_sample(q, k, v, lam_vecs, subln, lam_init, j=j):
                o = _attn_decode(q.reshape(bd, DA_SUB_HEADS, DA_HEAD_DIM), cache_k, cache_v, j, page_table,
                                 k.reshape(bd, DA_SUB_HEADS, DA_HEAD_DIM),
                                 v.reshape(bd, DA_HEADS, DA_V_DIM), lam_vecs, subln, lam_init)
                return o.reshape(bd, DA_HEADS * DA_V_DIM)

            xp, kp, vp = _attention_layer(xp, tm_p, i, j, w, rope_p, attend_prompt, False)
            xs, ks, vs = _attention_layer(xs, tm_s, i, j, w, rope_s, attend_sample, True)
            kp_rows.append(kp.reshape(b, s, DA_SUB_HEADS, DA_HEAD_DIM))
            vp_rows.append(vp.reshape(b, s, DA_HEADS, DA_V_DIM))
            ks_rows.append(ks.reshape(bd, t_dec, DA_SUB_HEADS, DA_HEAD_DIM))
            vs_rows.append(vs.reshape(bd, t_dec, DA_HEADS, DA_V_DIM))
        else:
            zero_buf = jnp.zeros((b * SUBLANES, GD_CONV_CH), F32)
            zero_state = jnp.zeros((b, GD_V_HEADS, GD_HEAD_DIM, GD_HEAD_DIM), F32)
            xp, sp, proj_p = _deltanet_layer(xp, tm_p, i, j, w, b, s, zero_buf, zero_state, False)
            buf_s = jnp.pad(state_conv[j], ((0, 0), (SUBLANES - (CONV_WIDTH - 1), 0), (0, 0)))
            xs, ss, proj_s = _deltanet_layer(xs, tm_s, i, j, w, bd, t_dec,
                                             buf_s.reshape(bd * SUBLANES, GD_CONV_CH), state_delta[j], True)
            sp_fin.append(sp)
            ss_fin.append(ss)
            cp_fin.append(proj_p.reshape(b, s, -1)[:, s - (CONV_WIDTH - 1):, :GD_CONV_CH])
            cs_all = jnp.concatenate([state_conv[j], proj_s.reshape(bd, t_dec, -1)[:, :, :GD_CONV_CH]], axis=1)
            cs_fin.append(cs_all[:, t_dec:])
    y_prompt = _rmsnorm(xp, norm_final, F32, tm_p).reshape(b, s, d)
    y_sample = _rmsnorm(xs, norm_final, F32, tm_s).reshape(bd, t_dec, d)
    return (y_prompt, y_sample,
            jnp.stack(kp_rows), jnp.stack(vp_rows), jnp.stack(ks_rows), jnp.stack(vs_rows),
            jnp.stack(sp_fin), jnp.stack(cp_fin), jnp.stack(ss_fin), jnp.stack(cs_fin))
```

```python
import functools
import math

import numpy as np
import jax
import jax.numpy as jnp
from jax import lax
from jax.experimental import pallas as pl
from jax.experimental.pallas import tpu as pltpu

F32 = jnp.float32
BF16 = jnp.bfloat16

PAGE_SIZE = 128
DA_HEAD_DIM = 64
DA_V_DIM = 128
DA_HEADS = 16
DA_SUB_HEADS = 32
DA_QK_WIDTH = 2048
ROPE_THETA = 500000.0
ROT_DIM = 16
LAMBDA_INIT_BASE = 0.8
LAMBDA_INIT_AMP = 0.6
LAMBDA_INIT_RATE = 0.3
GD_K_HEADS = 16
GD_V_HEADS = 32
GD_HEAD_DIM = 128
GD_KEY_WIDTH = 2048
GD_VAL_WIDTH = 4096
GD_CONV_CH = 8192
CONV_WIDTH = 4
CHUNK = 64
N_EXPERTS = 8
NORM_EPS = 1e-6
L2_EPS = 1e-6

LANES = 128
SUBLANES = 8
VMEM_LIMIT_BYTES = 56 * 1024 * 1024


def _params(n_grid_dims, vmem=VMEM_LIMIT_BYTES):
    return pltpu.CompilerParams(dimension_semantics=("arbitrary",) * n_grid_dims,
                                vmem_limit_bytes=vmem)


def _sigmoid(x):
    return 1.0 / (1.0 + jnp.exp(-x))


def _silu(x):
    return x * _sigmoid(x)


def _dot(a, b):
    return jnp.dot(a, b, preferred_element_type=F32)


def _dot_nt(a, b):
    return lax.dot_general(a, b, (((1,), (1,)), ((), ())), preferred_element_type=F32)


def _dot_tn(a, b):
    return lax.dot_general(a, b, (((0,), (0,)), ((), ())), preferred_element_type=F32)


def _dot_f32(a, b):
    return jnp.dot(a, b, preferred_element_type=F32, precision=lax.Precision.HIGHEST)


def _split_bf16(x):
    hi = x.astype(BF16)
    return hi, (x - hi.astype(F32)).astype(BF16)


def _rmsnorm_kernel(*refs):
    w_ref, o_ref = refs[-2:]
    x = refs[0][...]
    for extra in refs[1:-2]:
        x = x + extra[...]
    ms = jnp.mean(x * x, axis=-1, keepdims=True)
    o_ref[...] = (x * lax.rsqrt(ms + NORM_EPS) * w_ref[...]).astype(o_ref.dtype)


def _rmsnorm(xs, w, out_dtype, tm):
    m, d = xs[0].shape
    return pl.pallas_call(
        _rmsnorm_kernel,
        grid=(m // tm,),
        in_specs=[pl.BlockSpec((tm, d), lambda i: (i, 0)) for _ in xs]
                 + [pl.BlockSpec((1, d), lambda i: (0, 0))],
        out_specs=pl.BlockSpec((tm, d), lambda i: (i, 0)),
        out_shape=jax.ShapeDtypeStruct((m, d), out_dtype),
        compiler_params=_params(1),
        name="rmsnorm",
    )(*xs, w.reshape(1, d))


def _mm_kernel(*refs, n_w, n_x, n_o, epilogue, precise):
    a_ref = refs[0]
    w_refs = refs[1:1 + n_w]
    x_refs = refs[1 + n_w:1 + n_w + n_x]
    o_refs = refs[1 + n_w + n_x:1 + n_w + n_x + n_o]
    wbf_refs = refs[1 + n_w + n_x + n_o:]

    a = a_ref[...]
    if precise:
        accs = [_dot_f32(a, w_ref[...]) for w_ref in w_refs]
    else:
        @pl.when(pl.program_id(1) == 0)
        def _():
            for w_ref, wbf_ref in zip(w_refs, wbf_refs):
                wbf_ref[...] = w_ref[...].astype(BF16)

        accs = [_dot(a, wbf_ref[...]) for wbf_ref in wbf_refs]
    for o_ref, val in zip(o_refs, epilogue(accs, x_refs)):
        o_ref[...] = val.astype(o_ref.dtype)


def _matmul(a, w, widx, col_starts, n_cols, tm, tn, epilogue, extras, out_dtypes, name):
    m, k = a.shape
    precise = a.dtype == F32
    in_specs = [pl.BlockSpec((tm, k), lambda j, i: (i, 0))]
    for cs in col_starts:
        in_specs.append(pl.BlockSpec((None, k, tn), lambda j, i, cb=cs // tn: (widx, 0, cb + j)))
    for _, bs, im in extras:
        in_specs.append(pl.BlockSpec(bs, im))
    kernel = functools.partial(_mm_kernel, n_w=len(col_starts), n_x=len(extras),
                               n_o=len(out_dtypes), epilogue=epilogue, precise=precise)
    outs = pl.pallas_call(
        kernel,
        grid=(n_cols // tn, m // tm),
        in_specs=in_specs,
        out_specs=[pl.BlockSpec((tm, tn), lambda j, i: (i, j)) for _ in out_dtypes],
        out_shape=[jax.ShapeDtypeStruct((m, n_cols), dt) for dt in out_dtypes],
        scratch_shapes=[] if precise else [pltpu.VMEM((k, tn), BF16) for _ in col_starts],
        compiler_params=_params(2),
        name=name,
    )(a, *([w] * len(col_starts)), *[e[0] for e in extras])
    return outs


def _epi_plain(n_out):
    return lambda accs, xs: [accs[0]] * n_out


def _epi_residual(accs, xs):
    return [xs[0][...] + accs[0]]


def _epi_swiglu(accs, xs):
    return [_silu(accs[0]) * accs[1]]


def _epi_gated_residual(expert):
    def epi(accs, xs):
        gate = xs[1][...][:, expert:expert + 1]
        return [xs[0][...] + gate * accs[0]]
    return epi


def _apply_rope(acc, cos_ref, sin_ref):
    cos = cos_ref[...]
    sin = sin_ref[...]
    lane = lax.broadcasted_iota(jnp.int32, (1, LANES), 1) % DA_HEAD_DIM
    first_half = lane < ROT_DIM // 2
    cols = []
    for c in range(acc.shape[1] // LANES):
        x = acc[:, c * LANES:(c + 1) * LANES]
        partner = jnp.where(first_half, pltpu.roll(x, LANES - ROT_DIM // 2, 1),
                            pltpu.roll(x, ROT_DIM // 2, 1))
        cols.append(x * cos + partner * sin)
    return jnp.concatenate(cols, axis=1)


def _epi_rope(scales):
    def epi(accs, xs):
        r = _apply_rope(accs[0], xs[0], xs[1])
        return [r if s == 1.0 else r * s for s in scales]
    return epi


def _rope_table_kernel(inv_ref, sgn_ref, cos_ref, sin_ref, *, offset, period):
    rows = cos_ref.shape[0]
    t = lax.broadcasted_iota(jnp.int32, (rows, LANES), 0) % period
    pos = (t + offset).astype(F32)
    ang = pos * inv_ref[...]
    sgn = sgn_ref[...]
    cos_ref[...] = jnp.where(sgn != 0.0, jnp.cos(ang), 1.0)
    sin_ref[...] = sgn * jnp.sin(ang)


def _rope_tables(rows, offset, period):
    half = ROT_DIM // 2
    inv = ROPE_THETA ** (-2.0 * np.arange(half, dtype=np.float32) / ROT_DIM)
    lane = np.arange(LANES) % DA_HEAD_DIM
    inv_lane = np.where(lane < ROT_DIM, inv[lane % half], 0.0).astype(np.float32)
    sgn_lane = np.where(lane < half, -1.0, np.where(lane < ROT_DIM, 1.0, 0.0)).astype(np.float32)
    kernel = functools.partial(_rope_table_kernel, offset=offset, period=period)
    return pl.pallas_call(
        kernel,
        out_shape=[jax.ShapeDtypeStruct((rows, LANES), F32)] * 2,
        name="rope_tables",
    )(jnp.asarray(inv_lane).reshape(1, LANES), jnp.asarray(sgn_lane).reshape(1, LANES))


def _diff_lambda(lam_ref, lam_init):
    lv = lam_ref[...]
    a = jnp.sum(lv[0:1] * lv[1:2], axis=-1, keepdims=True)
    b = jnp.sum(lv[2:3] * lv[3:4], axis=-1, keepdims=True)
    return jnp.exp(a) - jnp.exp(b) + lam_init


def _sub_layer_norm(o, subln_ref, lam_init):
    ms = jnp.mean(o * o, axis=-1, keepdims=True)
    return o * lax.rsqrt(ms + NORM_EPS) * subln_ref[...] * (1.0 - lam_init)


def _attn_prompt_kernel(lam_ref, subln_ref, q_ref, k_ref, v_ref, o_ref, m_sc, l_sc, acc_sc,
                        *, tq, tk, lam_init):
    qi = pl.program_id(2)
    ki = pl.program_id(3)

    @pl.when(ki == 0)
    def _():
        m_sc[...] = jnp.full(m_sc.shape, -jnp.inf, F32)
        l_sc[...] = jnp.zeros(l_sc.shape, F32)
        acc_sc[...] = jnp.zeros(acc_sc.shape, F32)

    @pl.when(ki <= qi)
    def _():
        q = q_ref[...]
        k = k_ref[...]
        v = v_ref[...]
        lane = lax.broadcasted_iota(jnp.int32, (1, LANES), 1)
        row = lax.broadcasted_iota(jnp.int32, (tq, tk), 0) + qi * tq
        col = lax.broadcasted_iota(jnp.int32, (tq, tk), 1) + ki * tk
        causal = col <= row
        zero = jnp.zeros_like(q)
        for sub in range(2):
            in_sub = (lane < DA_HEAD_DIM) if sub == 0 else (lane >= DA_HEAD_DIM)
            s = _dot_nt(jnp.where(in_sub, q, zero), k)
            s = jnp.where(causal, s, -jnp.inf)
            m_prev = m_sc[sub]
            m_new = jnp.maximum(m_prev, jnp.max(s, axis=-1, keepdims=True))
            alpha = jnp.exp(m_prev - m_new)
            p = jnp.exp(s - m_new[:, :1])
            l_sc[sub] = alpha * l_sc[sub] + jnp.sum(p, axis=-1, keepdims=True)
            acc_sc[sub] = alpha * acc_sc[sub] + _dot(p.astype(BF16), v)
            m_sc[sub] = m_new

    @pl.when(ki == qi)
    def _():
        lam = _diff_lambda(lam_ref, lam_init)
        o = acc_sc[0] / l_sc[0] - lam * (acc_sc[1] / l_sc[1])
        o_ref[...] = _sub_layer_norm(o, subln_ref, lam_init).astype(o_ref.dtype)


def _attn_prompt(q, k, v, lam_vecs, subln, batch, seq, lam_init, tq=512):
    nq = seq // tq
    kernel = functools.partial(_attn_prompt_kernel, tq=tq, tk=tq, lam_init=lam_init)
    kv_spec = pl.BlockSpec((tq, LANES), lambda b, h, qi, ki: (b * nq + jnp.minimum(ki, qi), h))
    return pl.pallas_call(
        kernel,
        grid=(batch, DA_HEADS, nq, nq),
        in_specs=[pl.BlockSpec((SUBLANES, LANES), lambda b, h, qi, ki: (0, 0)),
                  pl.BlockSpec((1, LANES), lambda b, h, qi, ki: (0, 0)),
                  pl.BlockSpec((tq, LANES), lambda b, h, qi, ki: (b * nq + qi, h)),
                  kv_spec, kv_spec],
        out_specs=pl.BlockSpec((tq, LANES), lambda b, h, qi, ki: (b * nq + qi, h)),
        out_shape=jax.ShapeDtypeStruct((batch * seq, DA_HEADS * DA_V_DIM), BF16),
        scratch_shapes=[pltpu.VMEM((2, tq, LANES), F32), pltpu.VMEM((2, tq, LANES), F32),
                        pltpu.VMEM((2, tq, LANES), F32)],
        compiler_params=_params(4),
        name="attn_prompt",
    )(lam_vecs, subln, q, k, v)


def _attn_decode_kernel(pt_ref, lam_ref, subln_ref, spread_ref, qrep_ref, q_ref, kn_ref, vn_ref, *rest,
                        lam_init, n_blk):
    del pt_ref
    k_refs = rest[:n_blk]
    v_refs = rest[n_blk:2 * n_blk]
    o_ref, m_sc, l_sc, acc_sc = rest[2 * n_blk:]
    p = pl.program_id(1)

    @pl.when(p == 0)
    def _():
        m_sc[...] = jnp.full(m_sc.shape, -jnp.inf, F32)
        l_sc[...] = jnp.zeros(l_sc.shape, F32)
        acc_sc[...] = jnp.zeros(acc_sc.shape, F32)

    def update(s, weighted_values):
        m_prev = m_sc[...]
        m_new = jnp.maximum(m_prev, jnp.max(s, axis=-1, keepdims=True))
        alpha = jnp.exp(m_prev - m_new)
        pe = jnp.exp(s - m_new[:, :1])
        l_sc[...] = alpha * l_sc[...] + jnp.sum(pe, axis=-1, keepdims=True)
        acc_sc[...] = alpha * acc_sc[...] + weighted_values(pe)
        m_sc[...] = m_new

    width = PAGE_SIZE * DA_HEADS
    row_head = (lax.broadcasted_iota(jnp.int32, (2 * DA_SUB_HEADS, width), 0) % DA_SUB_HEADS) // 2
    col_head = lax.broadcasted_iota(jnp.int32, (2 * DA_SUB_HEADS, width), 1) % DA_HEADS
    own_head = row_head == col_head
    q_rep = qrep_ref[...]
    for k_ref, v_ref in zip(k_refs, v_refs):
        s = jnp.sum(k_ref[...] * q_rep, axis=1)

        def page_values(pe, v_ref=v_ref):
            p_hi, p_lo = _split_bf16(pe)
            spread = _dot(jnp.concatenate([p_hi, p_lo], axis=0), spread_ref[...])
            pm = jnp.where(own_head, spread, 0.0).astype(BF16)
            v_hi, v_lo = _split_bf16(v_ref[...].reshape(width, DA_V_DIM))
            both = _dot(pm, v_hi)
            return both[:DA_SUB_HEADS] + both[DA_SUB_HEADS:] + _dot(pm[:DA_SUB_HEADS], v_lo)

        update(s, page_values)

    @pl.when(p == pl.num_programs(1) - 1)
    def _():
        s_new = jnp.sum(q_ref[...] * kn_ref[...], axis=-1, keepdims=True)
        update(s_new, lambda pe: pe * vn_ref[...])
        lam = _diff_lambda(lam_ref, lam_init)
        acc_sc[...] = acc_sc[...] / l_sc[...]
        o = (acc_sc[pl.ds(0, DA_HEADS, stride=2), :]
             - lam * acc_sc[pl.ds(1, DA_HEADS, stride=2), :])
        o_ref[...] = _sub_layer_norm(o, subln_ref, lam_init).astype(o_ref.dtype)


def _attn_decode(q, cache_k, cache_v, layer, page_table, k_new, v_new, lam_vecs, subln, lam_init,
                 pages_per_step=4):
    bd, n_pages = page_table.shape
    n_blk = pages_per_step
    assert n_pages % n_blk == 0
    keys_t = jnp.transpose(cache_k, (0, 1, 3, 4, 2))
    q_rep = jnp.broadcast_to(q[..., None], q.shape + (PAGE_SIZE,))
    v_new2 = jnp.repeat(v_new, 2, axis=1)
    width = PAGE_SIZE * DA_HEADS
    spread = jnp.asarray(np.arange(width)[None, :] // DA_HEADS == np.arange(PAGE_SIZE)[:, None], BF16)
    kernel = functools.partial(_attn_decode_kernel, lam_init=lam_init, n_blk=n_blk)
    per_seq = lambda *dims: pl.BlockSpec((None,) + dims, lambda b, p, pt: (b,) + (0,) * len(dims))
    page = lambda dims, i: pl.BlockSpec(
        (None, None) + dims, lambda b, p, pt: (layer, pt[b, p * n_blk + i], 0, 0, 0))
    grid_spec = pltpu.PrefetchScalarGridSpec(
        num_scalar_prefetch=1,
        grid=(bd, n_pages // n_blk),
        in_specs=[
            pl.BlockSpec((SUBLANES, LANES), lambda b, p, pt: (0, 0)),
            pl.BlockSpec((1, LANES), lambda b, p, pt: (0, 0)),
            pl.BlockSpec((PAGE_SIZE, width), lambda b, p, pt: (0, 0)),
            per_seq(DA_SUB_HEADS, DA_HEAD_DIM, PAGE_SIZE),
            per_seq(DA_SUB_HEADS, DA_HEAD_DIM),
            per_seq(DA_SUB_HEADS, DA_HEAD_DIM),
            per_seq(DA_SUB_HEADS, DA_V_DIM),
        ] + [page((DA_SUB_HEADS, DA_HEAD_DIM, PAGE_SIZE), i) for i in range(n_blk)]
          + [page((PAGE_SIZE, DA_HEADS, DA_V_DIM), i) for i in range(n_blk)],
        out_specs=per_seq(DA_HEADS, DA_V_DIM),
        scratch_shapes=[pltpu.VMEM((DA_SUB_HEADS, LANES), F32), pltpu.VMEM((DA_SUB_HEADS, LANES), F32),
                        pltpu.VMEM((DA_SUB_HEADS, DA_V_DIM), F32)],
    )
    return pl.pallas_call(
        kernel,
        grid_spec=grid_spec,
        out_shape=jax.ShapeDtypeStruct((bd, DA_HEADS, DA_V_DIM), F32),
        compiler_params=_params(2),
        name="attn_decode",
    )(page_table, lam_vecs, subln, spread, q_rep, q, k_new, v_new2,
      *([keys_t] * n_blk), *([cache_v] * n_blk))


def _gdn_gates_kernel(ba_ref, alog_ref, dtb_ref, o_ref):
    x = ba_ref[...]
    lane = lax.broadcasted_iota(jnp.int32, x.shape, 1)
    beta = _sigmoid(x)
    y = x + dtb_ref[...]
    softplus = jnp.maximum(y, 0.0) + jnp.log1p(jnp.exp(-jnp.abs(y)))
    g = -jnp.exp(alog_ref[...]) * softplus
    o_ref[...] = jnp.where(lane < GD_V_HEADS, beta, jnp.where(lane < 2 * GD_V_HEADS, g, 0.0))


def _gdn_gates(ba, a_log, dt_bias, tm):
    m = ba.shape[0]
    pad = lambda v: jnp.pad(v.reshape(1, GD_V_HEADS), ((0, 0), (GD_V_HEADS, LANES - 2 * GD_V_HEADS)))
    return pl.pallas_call(
        _gdn_gates_kernel,
        grid=(m // tm,),
        in_specs=[pl.BlockSpec((tm, LANES), lambda i: (i, 0)),
                  pl.BlockSpec((1, LANES), lambda i: (0, 0)),
                  pl.BlockSpec((1, LANES), lambda i: (0, 0))],
        out_specs=pl.BlockSpec((tm, LANES), lambda i: (i, 0)),
        out_shape=jax.ShapeDtypeStruct((m, LANES), F32),
        compiler_params=_params(1),
        name="gdn_gates",
    )(ba, pad(a_log), pad(dt_bias))


def _gdn_conv_kernel(x_ref, buf_ref, w_ref, o_ref, xin_sc, *, heads):
    t = x_ref.shape[0]
    xin_sc[0:SUBLANES, :] = buf_ref[...]
    xin_sc[SUBLANES:SUBLANES + t, :] = x_ref[...]
    for hh in range(heads):
        c = pl.program_id(1) * heads + hh
        cols = slice(hh * LANES, (hh + 1) * LANES)
        w = w_ref[:, cols]
        y = x_ref[:, cols] * w[CONV_WIDTH - 1:CONV_WIDTH]
        for s in range(1, CONV_WIDTH):
            shifted = xin_sc[pl.ds(SUBLANES - s, t), cols]
            y = y + shifted * w[CONV_WIDTH - 1 - s:CONV_WIDTH - s]
        y = _silu(y)
        inv = lax.rsqrt(jnp.sum(y * y, axis=-1, keepdims=True) + L2_EPS)
        scale = jnp.where(c < GD_K_HEADS, GD_HEAD_DIM ** -0.5, 1.0)
        o_ref[:, cols] = jnp.where(c < 2 * GD_K_HEADS, y * (inv * scale), y)


def _gdn_conv(x, buf, conv_w, widx, batch, t, heads):
    width = heads * LANES
    return pl.pallas_call(
        functools.partial(_gdn_conv_kernel, heads=heads),
        grid=(batch, GD_CONV_CH // width),
        in_specs=[pl.BlockSpec((t, width), lambda b, c: (b, c)),
                  pl.BlockSpec((SUBLANES, width), lambda b, c: (b, c)),
                  pl.BlockSpec((None, CONV_WIDTH, width), lambda b, c: (widx, 0, c))],
        out_specs=pl.BlockSpec((t, width), lambda b, c: (b, c)),
        out_shape=jax.ShapeDtypeStruct((batch * t, GD_CONV_CH), F32),
        scratch_shapes=[pltpu.VMEM((SUBLANES + t, width), F32)],
        compiler_params=_params(2),
        name="gdn_conv",
    )(x, buf, conv_w)


def _split_products(lhs_list, rhs_list):
    ls = [_split_bf16(a) for a in lhs_list]
    rs = [_split_bf16(b) for b in rhs_list]
    first = [_dot(jnp.concatenate([a_hi, a_lo], axis=0), b_hi) for (a_hi, a_lo), (b_hi, _) in zip(ls, rs)]
    second = [_dot(a_hi, b_lo) for (a_hi, _), (_, b_lo) in zip(ls, rs)]
    outs = []
    for f, s, a in zip(first, second, lhs_list):
        n = a.shape[0]
        outs.append(f[:n] + f[n:] + s)
    return outs


def _unit_lower_inverses(ms, eye):
    n = eye.shape[0]
    invs = [eye - m for m in ms]
    powers = _split_products(ms, ms)
    n_factors = int(math.log2(CHUNK)) - 1
    for f in range(n_factors):
        last = f + 1 == n_factors
        lhs = invs if last else [jnp.concatenate([i, p], axis=0) for i, p in zip(invs, powers)]
        prods = _split_products(lhs, powers)
        invs = [i + pr[:n] for i, pr in zip(invs, prods)]
        if not last:
            powers = [pr[n:] for pr in prods]
    return invs


def _gdn_chunk_kernel(q_ref, k_ref, v_ref, z_ref, bg_ref, gt_ref, gn_ref, s0_ref, o_ref, sfin_ref,
                      s_sc, *, heads):
    grp = pl.program_id(1)
    c = pl.program_id(2)
    pairs = heads // 2
    two = 2 * CHUNK

    @pl.when(c == 0)
    def _():
        s_sc[...] = s0_ref[...].reshape(s_sc.shape)

    bg = bg_ref[...]
    gn = gn_ref[...]
    lane = lax.broadcasted_iota(jnp.int32, (1, LANES), 1)
    ri = lax.broadcasted_iota(jnp.int32, (two, two), 0)
    ci = lax.broadcasted_iota(jnp.int32, (two, two), 1)
    same = (ri // CHUNK) == (ci // CHUNK)
    tril = same & (ri >= ci)
    triu = same & (ri <= ci)
    strict = same & (ri > ci)
    eye = (ri == ci).astype(F32)
    top_rows = lax.broadcasted_iota(jnp.int32, (two, 1), 0) < CHUNK
    top_state = lax.broadcasted_iota(jnp.int32, (2 * GD_HEAD_DIM, 1), 0) < GD_HEAD_DIM

    def stack2(x):
        return jnp.concatenate([x, x], axis=0)

    def blocks(x):
        zero = jnp.zeros_like(x)
        return jnp.concatenate([jnp.where(top_rows, x, zero), jnp.where(top_rows, zero, x)], axis=1)

    def column(lane_idx):
        return jnp.sum(jnp.where(lane == lane_idx, bg, 0.0), axis=-1, keepdims=True)

    qs = [stack2(q_ref[:, kh * LANES:(kh + 1) * LANES]) for kh in range(pairs)]
    ks = [stack2(k_ref[:, kh * LANES:(kh + 1) * LANES]) for kh in range(pairs)]
    vs = [jnp.concatenate([v_ref[:, (2 * kh) * LANES:(2 * kh + 1) * LANES],
                           v_ref[:, (2 * kh + 1) * LANES:(2 * kh + 2) * LANES]], axis=0)
          for kh in range(pairs)]
    kq = [_dot_nt(jnp.concatenate([k, q], axis=0).astype(BF16), k.astype(BF16)) for k, q in zip(ks, qs)]

    betas, gcums, decays, egs = [], [], [], []
    for kh in range(pairs):
        h = grp * heads + 2 * kh
        beta = jnp.concatenate([column(h), column(h + 1)], axis=0)
        g_col = jnp.concatenate([column(h + GD_V_HEADS), column(h + 1 + GD_V_HEADS)], axis=0)
        g_row = gt_ref[kh:kh + 1, :]
        gcum_col = jnp.sum(jnp.where(tril, g_row, 0.0), axis=-1, keepdims=True)
        gcum_row = jnp.sum(jnp.where(triu, g_col, 0.0), axis=0, keepdims=True)
        betas.append(beta)
        gcums.append(gcum_col)
        decays.append(jnp.exp(jnp.where(tril, gcum_col - gcum_row, -jnp.inf)))
        egs.append(jnp.exp(gcum_col))

    tinvs = _unit_lower_inverses(
        [jnp.where(strict, b * x[:two] * d, 0.0) for b, x, d in zip(betas, kq, decays)], eye)
    rhs = [jnp.concatenate([v * b, k * (b * e)], axis=1).astype(BF16)
           for v, k, b, e in zip(vs, ks, betas, egs)]
    wus = [_dot(t.astype(BF16), r) for t, r in zip(tinvs, rhs)]
    s_olds = [s_sc[kh] for kh in range(pairs)]
    s_bfs = [s.astype(BF16) for s in s_olds]
    v_news = [wu[:, :LANES] - _dot(blocks(wu[:, LANES:]).astype(BF16), s)
              for wu, s in zip(wus, s_bfs)]
    v_bfs = [v.astype(BF16) for v in v_news]
    outs = [_dot(jnp.concatenate([blocks(q * e), x[two:] * d], axis=1).astype(BF16),
                 jnp.concatenate([s, v], axis=0))
            for q, e, x, d, s, v in zip(qs, egs, kq, decays, s_bfs, v_bfs)]
    g_lasts = [jnp.where(top_rows, g[CHUNK - 1:CHUNK, :], g[two - 1:two, :]) for g in gcums]
    updates = [_dot_tn(blocks(k * jnp.exp(gl - g)).astype(BF16), v)
               for k, gl, g, v in zip(ks, g_lasts, gcums, v_bfs)]
    for kh in range(pairs):
        g = gcums[kh]
        carry = jnp.exp(jnp.where(top_state, g[CHUNK - 1:CHUNK, :], g[two - 1:two, :]))
        s_sc[kh] = s_olds[kh] * carry + updates[kh]
        o = outs[kh]
        z = jnp.concatenate([z_ref[:, (2 * kh) * LANES:(2 * kh + 1) * LANES],
                             z_ref[:, (2 * kh + 1) * LANES:(2 * kh + 2) * LANES]], axis=0)
        ms = jnp.mean(o * o, axis=-1, keepdims=True)
        on = (o * lax.rsqrt(ms + NORM_EPS) * gn * _silu(z)).astype(o_ref.dtype)
        o_ref[:, (2 * kh) * LANES:(2 * kh + 1) * LANES] = on[:CHUNK]
        o_ref[:, (2 * kh + 1) * LANES:(2 * kh + 2) * LANES] = on[CHUNK:]

    @pl.when(c == pl.num_programs(2) - 1)
    def _():
        sfin_ref[...] = s_sc[...].reshape(sfin_ref.shape)


def _gdn_chunks(qkv, proj, bg, g_t, gnorm, s0, batch, t, heads=16):
    nc = t // CHUNK
    n_grp = GD_V_HEADS // heads
    pairs = heads // 2
    kw = pairs * LANES
    vw = heads * LANES
    row = lambda b, g, c: b * nc + c
    kernel = functools.partial(_gdn_chunk_kernel, heads=heads)
    return pl.pallas_call(
        kernel,
        grid=(batch, n_grp, nc),
        in_specs=[
            pl.BlockSpec((CHUNK, kw), lambda b, g, c: (row(b, g, c), g)),
            pl.BlockSpec((CHUNK, kw), lambda b, g, c: (row(b, g, c), GD_KEY_WIDTH // kw + g)),
            pl.BlockSpec((CHUNK, vw), lambda b, g, c: (row(b, g, c), 2 * GD_KEY_WIDTH // vw + g)),
            pl.BlockSpec((CHUNK, vw), lambda b, g, c: (row(b, g, c), GD_CONV_CH // vw + g)),
            pl.BlockSpec((CHUNK, LANES), lambda b, g, c: (row(b, g, c), 0)),
            pl.BlockSpec((None, None, pairs, 2 * CHUNK), lambda b, g, c: (b, c, g, 0)),
            pl.BlockSpec((1, LANES), lambda b, g, c: (0, 0)),
            pl.BlockSpec((None, heads, GD_HEAD_DIM, GD_HEAD_DIM), lambda b, g, c: (b, g, 0, 0)),
        ],
        out_specs=[
            pl.BlockSpec((CHUNK, vw), lambda b, g, c: (row(b, g, c), g)),
            pl.BlockSpec((None, heads, GD_HEAD_DIM, GD_HEAD_DIM), lambda b, g, c: (b, g, 0, 0)),
        ],
        out_shape=[jax.ShapeDtypeStruct((batch * t, GD_VAL_WIDTH), BF16),
                   jax.ShapeDtypeStruct((batch, GD_V_HEADS, GD_HEAD_DIM, GD_HEAD_DIM), F32)],
        scratch_shapes=[pltpu.VMEM((pairs, 2 * GD_HEAD_DIM, GD_HEAD_DIM), F32)],
        compiler_params=_params(3),
        name="gdn_chunks",
    )(qkv, qkv, qkv, proj, bg, g_t, gnorm, s0)


def _gdn_step_kernel(qkv_ref, z_ref, bg_ref, gn_ref, s0_ref, o_ref, sfin_ref):
    ri = lax.broadcasted_iota(jnp.int32, (GD_HEAD_DIM, GD_HEAD_DIM), 0)
    ci = lax.broadcasted_iota(jnp.int32, (GD_HEAD_DIM, GD_HEAD_DIM), 1)
    diag = ri == ci
    bg = bg_ref[...]
    gn = gn_ref[...]

    def column(row_vec):
        return jnp.sum(jnp.where(diag, row_vec, 0.0), axis=-1, keepdims=True)

    for kh in range(GD_K_HEADS):
        q_col = column(qkv_ref[:, kh * LANES:(kh + 1) * LANES])
        k_col = column(qkv_ref[:, GD_KEY_WIDTH + kh * LANES:GD_KEY_WIDTH + (kh + 1) * LANES])
        for hh in (2 * kh, 2 * kh + 1):
            v = qkv_ref[:, 2 * GD_KEY_WIDTH + hh * LANES:2 * GD_KEY_WIDTH + (hh + 1) * LANES]
            beta = bg[:, hh:hh + 1]
            eg = jnp.exp(bg[:, GD_V_HEADS + hh:GD_V_HEADS + hh + 1])
            s_old = s0_ref[hh]
            sk = jnp.sum(s_old * k_col, axis=0, keepdims=True)
            v_new = beta * (v - eg * sk)
            s_new = eg * s_old + k_col * v_new
            sfin_ref[hh] = s_new
            o = jnp.sum(s_new * q_col, axis=0, keepdims=True)
            z = z_ref[:, hh * LANES:(hh + 1) * LANES]
            ms = jnp.mean(o * o, axis=-1, keepdims=True)
            o_ref[:, hh * LANES:(hh + 1) * LANES] = o * lax.rsqrt(ms + NORM_EPS) * gn * _silu(z)


def _gdn_step(qkv, z, bg, gnorm, s0):
    batch = qkv.shape[0]
    row = lambda width: pl.BlockSpec((None, 1, width), lambda b: (b, 0, 0))
    state = pl.BlockSpec((None, GD_V_HEADS, GD_HEAD_DIM, GD_HEAD_DIM), lambda b: (b, 0, 0, 0))
    return pl.pallas_call(
        _gdn_step_kernel,
        grid=(batch,),
        in_specs=[row(GD_CONV_CH), row(GD_VAL_WIDTH), row(LANES),
                  pl.BlockSpec((1, LANES), lambda b: (0, 0)), state],
        out_specs=[row(GD_VAL_WIDTH), state],
        out_shape=[jax.ShapeDtypeStruct((batch, 1, GD_VAL_WIDTH), F32),
                   jax.ShapeDtypeStruct(s0.shape, F32)],
        compiler_params=_params(1),
        name="gdn_step",
    )(qkv, z, bg, gnorm, s0)


def _router_kernel(h_ref, w_ref, o_ref):
    if h_ref.dtype == F32:
        logits = _dot_f32(h_ref[...], w_ref[...])
    else:
        logits = _dot(h_ref[...], w_ref[...].astype(BF16))
    e = jnp.exp(logits - jnp.max(logits, axis=-1, keepdims=True))
    probs = e / jnp.sum(e, axis=-1, keepdims=True)
    lane = lax.broadcasted_iota(jnp.int32, probs.shape, 1)
    v1 = jnp.max(probs, axis=-1, keepdims=True)
    i1 = jnp.min(jnp.where(probs == v1, lane, N_EXPERTS), axis=-1, keepdims=True)
    rest = jnp.where(lane == i1, -1.0, probs)
    v2 = jnp.max(rest, axis=-1, keepdims=True)
    i2 = jnp.min(jnp.where(rest == v2, lane, N_EXPERTS), axis=-1, keepdims=True)
    total = v1 + v2
    o_ref[...] = jnp.where(lane == i1, v1 / total, 0.0) + jnp.where(lane == i2, v2 / total, 0.0)


def _router(h, w_router, widx, tm):
    m, d = h.shape
    return pl.pallas_call(
        _router_kernel,
        grid=(m // tm,),
        in_specs=[pl.BlockSpec((tm, d), lambda i: (i, 0)),
                  pl.BlockSpec((None, d, N_EXPERTS), lambda i: (widx, 0, 0))],
        out_specs=pl.BlockSpec((tm, N_EXPERTS), lambda i: (i, 0)),
        out_shape=jax.ShapeDtypeStruct((m, N_EXPERTS), F32),
        compiler_params=_params(1),
        name="moe_router",
    )(h, w_router)


MOE_BLOCK = 1024
MOE_SUB = 128
MOE_FF_CHUNK = 256


def _moe_kernel(cnt_ref, h_ref, gc_ref, gr_ref, wg_ref, wu_ref, wd_ref, o_ref,
                xg_sc, y_sc, rc_sc, rr_sc, gate_sc):
    i = pl.program_id(0)
    e = pl.program_id(1)
    f = pl.program_id(2)
    tb = h_ref.shape[0]
    n_sub = tb // MOE_SUB
    cnt = cnt_ref[i, e]
    rows = lambda s: slice(s * MOE_SUB, (s + 1) * MOE_SUB)

    @pl.when((e == 0) & (f == 0))
    def _():
        o_ref[...] = jnp.zeros(o_ref.shape, F32)
        ri = lax.broadcasted_iota(jnp.int32, (MOE_SUB, MOE_SUB), 0)
        ci = lax.broadcasted_iota(jnp.int32, (MOE_SUB, MOE_SUB), 1)
        before_c = jnp.where(ci < ri, 1.0, 0.0).astype(BF16)
        before_r = jnp.where(ri < ci, 1.0, 0.0).astype(BF16)
        off_c = jnp.zeros((1, N_EXPERTS), F32)
        off_r = jnp.zeros((N_EXPERTS, 1), F32)
        for s in range(n_sub):
            sel_c = jnp.where(gc_ref[rows(s), :] > 0.0, 1.0, 0.0)
            rc_sc[rows(s), :] = _dot(before_c, sel_c.astype(BF16)) + off_c
            off_c = off_c + jnp.sum(sel_c, axis=0, keepdims=True)
            sel_r = jnp.where(gr_ref[:, rows(s)] > 0.0, 1.0, 0.0)
            rr_sc[:, rows(s)] = _dot(sel_r.astype(BF16), before_r) + off_r
            off_r = off_r + jnp.sum(sel_r, axis=1, keepdims=True)

    @pl.when(f == 0)
    def _():
        rank_r = rr_sc[pl.ds(e, 1), :]
        gate_r = gr_ref[pl.ds(e, 1), :]
        for s in range(n_sub):
            @pl.when(s * MOE_SUB < cnt)
            def _():
                slot = (lax.broadcasted_iota(jnp.int32, (MOE_SUB, 1), 0) + s * MOE_SUB).astype(F32)
                pick = (rank_r == slot) & (gate_r > 0.0)
                xg_sc[rows(s), :] = _dot(jnp.where(pick, 1.0, 0.0).astype(BF16),
                                         h_ref[...]).astype(BF16)
                gate_sc[rows(s), :] = jnp.sum(jnp.where(pick, gate_r, 0.0), axis=1, keepdims=True)
                y_sc[rows(s), :] = jnp.zeros((MOE_SUB, y_sc.shape[1]), F32)

    for s in range(n_sub):
        @pl.when(s * MOE_SUB < cnt)
        def _():
            xs = xg_sc[rows(s), :]
            hid = _silu(_dot(xs, wg_ref[...])) * _dot(xs, wu_ref[...])
            y_sc[rows(s), :] += _dot(hid.astype(BF16), wd_ref[...])

    @pl.when(f == pl.num_programs(2) - 1)
    def _():
        lane = lax.broadcasted_iota(jnp.int32, (1, N_EXPERTS), 1)
        rank_c = jnp.sum(jnp.where(lane == e, rc_sc[...], 0.0), axis=1, keepdims=True)
        gate_c = jnp.sum(jnp.where(lane == e, gc_ref[...], 0.0), axis=1, keepdims=True)
        for s in range(n_sub):
            @pl.when(s * MOE_SUB < cnt)
            def _():
                y_hi, y_lo = _split_bf16(y_sc[rows(s), :] * gate_sc[rows(s), :])
                slot = (lax.broadcasted_iota(jnp.int32, (1, MOE_SUB), 1) + s * MOE_SUB).astype(F32)
                place = jnp.where((rank_c == slot) & (gate_c > 0.0), 1.0, 0.0).astype(BF16)
                o_ref[...] += _dot(jnp.concatenate([place, place], axis=1),
                                   jnp.concatenate([y_hi, y_lo], axis=0))


def _moe_routed(h, gates, w_gu, w_down):
    m, d = h.shape
    n_exp, f_dim, _ = w_down.shape
    nb = m // MOE_BLOCK
    nf = f_dim // MOE_FF_CHUNK
    counts = jnp.sum((gates > 0.0).reshape(nb, MOE_BLOCK, n_exp), axis=1).astype(jnp.int32)
    grid_spec = pltpu.PrefetchScalarGridSpec(
        num_scalar_prefetch=1,
        grid=(nb, n_exp, nf),
        in_specs=[
            pl.BlockSpec((MOE_BLOCK, d), lambda i, e, f, c: (i, 0)),
            pl.BlockSpec((MOE_BLOCK, n_exp), lambda i, e, f, c: (i, 0)),
            pl.BlockSpec((n_exp, MOE_BLOCK), lambda i, e, f, c: (0, i)),
            pl.BlockSpec((None, d, MOE_FF_CHUNK), lambda i, e, f, c: (e, 0, f)),
            pl.BlockSpec((None, d, MOE_FF_CHUNK), lambda i, e, f, c: (e, 0, nf + f)),
            pl.BlockSpec((None, MOE_FF_CHUNK, d), lambda i, e, f, c: (e, f, 0)),
        ],
        out_specs=pl.BlockSpec((MOE_BLOCK, d), lambda i, e, f, c: (i, 0)),
        scratch_shapes=[pltpu.VMEM((MOE_BLOCK, d), BF16), pltpu.VMEM((MOE_BLOCK, d), F32),
                        pltpu.VMEM((MOE_BLOCK, n_exp), F32), pltpu.VMEM((n_exp, MOE_BLOCK), F32),
                        pltpu.VMEM((MOE_BLOCK, 1), F32)],
    )
    return pl.pallas_call(
        _moe_kernel,
        grid_spec=grid_spec,
        out_shape=jax.ShapeDtypeStruct((m, d), F32),
        compiler_params=_params(3),
        name="moe_routed",
    )(counts, h, gates, gates.T, w_gu.astype(BF16), w_gu.astype(BF16), w_down.astype(BF16))


def _lambda_vectors(lq1, lk1, lq2, lk2):
    rows = jnp.stack([lq1, lk1, lq2, lk2]).astype(F32)
    return jnp.pad(rows, ((0, SUBLANES - 4), (0, LANES - DA_HEAD_DIM)))


def _attention_layer(x, tm, layer, j, w, rope, attend, decode):
    d = x.shape[1]
    act = F32 if decode else BF16
    lam_init = LAMBDA_INIT_BASE - LAMBDA_INIT_AMP * math.exp(-LAMBDA_INIT_RATE * layer)
    lam_vecs = _lambda_vectors(w['lambda_q1'][j], w['lambda_k1'][j], w['lambda_q2'][j], w['lambda_k2'][j])
    subln = w['subln_da'][j].reshape(1, DA_V_DIM)
    cos, sin, n_tab = rope
    tabs = [(arr, (tm, LANES), lambda jj, i: (i % n_tab, 0)) for arr in (cos, sin)]
    h = _rmsnorm([x], w['norm_mix'][layer], act, tm)
    tn = 512
    kv_dtypes = [F32] if decode else [F32, BF16]
    (q,) = _matmul(h, w['w_qkv_da'], j, [0], DA_QK_WIDTH, tm, tn,
                   _epi_rope([DA_HEAD_DIM ** -0.5]), tabs, [act], "da_q")
    k = _matmul(h, w['w_qkv_da'], j, [DA_QK_WIDTH], DA_QK_WIDTH, tm, tn,
                _epi_rope([1.0] * len(kv_dtypes)), tabs, kv_dtypes, "da_k")
    v = _matmul(h, w['w_qkv_da'], j, [2 * DA_QK_WIDTH], DA_HEADS * DA_V_DIM, tm, tn,
                _epi_plain(len(kv_dtypes)), [], kv_dtypes, "da_v")
    o = attend(q, k[-1], v[-1], lam_vecs, subln, lam_init)
    res = (x, (tm, tn), lambda jj, i: (i, jj))
    (x,) = _matmul(o, w['w_o_da'], j, [0], d, tm, tn, _epi_residual, [res], [F32], "da_out")
    h = _rmsnorm([x], w['norm_ffn'][layer], act, tm)
    d_ff = w['w_down_dense'].shape[1]
    (hid,) = _matmul(h, w['w_gu_dense'], j, [0, d_ff], d_ff, tm, tn, _epi_swiglu, [], [act], "ffn_gu")
    tm_down = min(tm, 256)
    res = (x, (tm_down, tn), lambda jj, i: (i, jj))
    (x,) = _matmul(hid, w['w_down_dense'], j, [0], d, tm_down, tn, _epi_residual, [res], [F32], "ffn_down")
    return x, k[0], v[0]


def _deltanet_layer(x, tm, layer, j, w, batch, t, conv_buf, s0, decode):
    d = x.shape[1]
    act = F32 if decode else BF16
    h = _rmsnorm([x], w['norm_mix'][layer], act, tm)
    n_main = GD_CONV_CH + GD_VAL_WIDTH
    (proj,) = _matmul(h, w['w_in_gd'], j, [0], n_main, tm, 512, _epi_plain(1), [], [F32], "gd_in")
    (ba,) = _matmul(h, w['w_in_gd'], j, [n_main], LANES, tm, LANES, _epi_plain(1), [], [F32], "gd_in_ba")
    bg = _gdn_gates(ba, w['a_log_gd'][j], w['dt_bias_gd'][j], tm)
    gnorm = w['gnorm_gd'][j].reshape(1, GD_HEAD_DIM)
    if decode:
        conv_in = jnp.pad(proj.reshape(batch, 1, -1), ((0, 0), (0, SUBLANES - 1), (0, 0)))
        qkv = _gdn_conv(conv_in.reshape(batch * SUBLANES, -1), conv_buf, w['conv_w_gd'], j, batch,
                        SUBLANES, GD_K_HEADS)
        qkv = qkv.reshape(batch, SUBLANES, -1)[:, :1]
        z = proj[:, GD_CONV_CH:n_main].reshape(batch, 1, GD_VAL_WIDTH)
        o, s_fin = _gdn_step(qkv, z, bg.reshape(batch, 1, LANES), gnorm, s0)
        o = o.reshape(batch, GD_VAL_WIDTH)
    else:
        qkv = _gdn_conv(proj, conv_buf, w['conv_w_gd'], j, batch, t, 1)
        g_t = bg[:, GD_V_HEADS:2 * GD_V_HEADS].reshape(batch, t // CHUNK, CHUNK, GD_K_HEADS, 2)
        g_t = g_t.transpose(0, 1, 3, 4, 2).reshape(batch, t // CHUNK, GD_K_HEADS, 2 * CHUNK)
        o, s_fin = _gdn_chunks(qkv, proj, bg, g_t, gnorm, s0, batch, t)
    res = (x, (tm, 512), lambda jj, i: (i, jj))
    (x,) = _matmul(o, w['w_o_gd'], j, [0], d, tm, 512, _epi_residual, [res], [F32], "gd_out")
    h = _rmsnorm([x], w['norm_ffn'][layer], act, tm)
    gates = _router(h, w['w_router'], j, tm)
    w_gu = w['w_gu_moe'][j]
    w_down = w['w_down_moe'][j]
    d_ffe = w_down.shape[1]
    if not decode:
        return x, _moe_routed(h, gates, w_gu, w_down), s_fin, proj
    h = h.astype(BF16)
    gate_x = (gates, (tm, N_EXPERTS), lambda jj, i: (i, 0))
    for e in range(N_EXPERTS):
        (hid,) = _matmul(h, w_gu, e, [0, d_ffe], d_ffe, tm, 256, _epi_swiglu, [], [BF16], "moe_gu")
        res = (x, (tm, 512), lambda jj, i: (i, jj))
        (x,) = _matmul(hid, w_down, e, [0], d, tm, 512, _epi_gated_residual(e), [res, gate_x], [F32],
                       "moe_down")
    return x, None, s_fin, proj


def kernel(x_prompt, x_sample, cache_k, cache_v, state_delta, state_conv, page_table, norm_mix, norm_ffn, norm_final, w_qkv_da, lambda_q1, lambda_k1, lambda_q2, lambda_k2, subln_da, w_o_da, w_in_gd, conv_w_gd, a_log_gd, dt_bias_gd, gnorm_gd, w_o_gd, w_gu_dense, w_down_dense, w_router, w_gu_moe, w_down_moe):
    w = dict(norm_mix=norm_mix, norm_ffn=norm_ffn, w_qkv_da=w_qkv_da, lambda_q1=lambda_q1,
             lambda_k1=lambda_k1, lambda_q2=lambda_q2, lambda_k2=lambda_k2, subln_da=subln_da,
             w_o_da=w_o_da, w_in_gd=w_in_gd, conv_w_gd=conv_w_gd, a_log_gd=a_log_gd,
             dt_bias_gd=dt_bias_gd, gnorm_gd=gnorm_gd, w_o_gd=w_o_gd, w_gu_dense=w_gu_dense,
             w_down_dense=w_down_dense, w_router=w_router, w_gu_moe=w_gu_moe, w_down_moe=w_down_moe)
    b, s, d = x_prompt.shape
    bd, t_dec, _ = x_sample.shape
    assert t_dec == 1
    n_pages = page_table.shape[1]
    past = n_pages * PAGE_SIZE
    tm_p = 512
    tm_s = bd * t_dec
    depth = norm_mix.shape[0]

    xp = x_prompt.reshape(b * s, d)
    xs = x_sample.reshape(bd * t_dec, d)
    cos_p, sin_p = _rope_tables(s, 0, s)
    cos_s, sin_s = _rope_tables(tm_s, past, t_dec)
    rope_p = (cos_p, sin_p, s // tm_p)
    rope_s = (cos_s, sin_s, 1)

    kp_rows, vp_rows, ks_rows, vs_rows = [], [], [], []
    sp_fin, cp_fin, ss_fin, cs_fin = [], [], [], []
    xp_terms = None
    for i in range(depth):
        j = i // 2
        if i % 2 == 0:
            def attend_prompt(q, k, v, lam_vecs, subln, lam_init):
                return _attn_prompt(q, k, v, lam_vecs, subln, b, s, lam_init)

            def attend_sample(q, k, v, lam_vecs, subln, lam_init, j=j):
                o = _attn_decode(q.reshape(bd, DA_SUB_HEADS, DA_HEAD_DIM), cache_k, cache_v, j, page_table,
                                 k.reshape(bd, DA_SUB_HEADS, DA_HEAD_DIM),
                                 v.reshape(bd, DA_HEADS, DA_V_DIM), lam_vecs, subln, lam_init)
                return o.reshape(bd, DA_HEADS * DA_V_DIM)

            xp, kp, vp = _attention_layer(xp, tm_p, i, j, w, rope_p, attend_prompt, False)
            xs, ks, vs = _attention_layer(xs, tm_s, i, j, w, rope_s, attend_sample, True)
            kp_rows.append(kp.reshape(b, s, DA_SUB_HEADS, DA_HEAD_DIM))
            vp_rows.append(vp.reshape(b, s, DA_HEADS, DA_V_DIM))
            ks_rows.append(ks.reshape(bd, t_dec, DA_SUB_HEADS, DA_HEAD_DIM))
            vs_rows.append(vs.reshape(bd, t_dec, DA_HEADS, DA_V_DIM))
        else:
            zero_buf = jnp.zeros((b * SUBLANES, GD_CONV_CH), F32)
            zero_state = jnp.zeros((b, GD_V_HEADS, GD_HEAD_DIM, GD_HEAD_DIM), F32)
            assert i == depth - 1
            xp, moe_p, sp, proj_p = _deltanet_layer(xp, tm_p, i, j, w, b, s, zero_buf, zero_state, False)
            xp_terms = [xp, moe_p]
            buf_s = jnp.pad(state_conv[j], ((0, 0), (SUBLANES - (CONV_WIDTH - 1), 0), (0, 0)))
            xs, _, ss, proj_s = _deltanet_layer(xs, tm_s, i, j, w, bd, t_dec,
                                                buf_s.reshape(bd * SUBLANES, GD_CONV_CH), state_delta[j], True)
            sp_fin.append(sp)
            ss_fin.append(ss)
            cp_fin.append(proj_p.reshape(b, s, -1)[:, s - (CONV_WIDTH - 1):, :GD_CONV_CH])
            cs_all = jnp.concatenate([state_conv[j], proj_s.reshape(bd, t_dec, -1)[:, :, :GD_CONV_CH]], axis=1)
            cs_fin.append(cs_all[:, t_dec:])
    y_prompt = _rmsnorm(xp_terms or [xp], norm_final, F32, tm_p).reshape(b, s, d)
    y_sample = _rmsnorm([xs], norm_final, F32, tm_s).reshape(bd, t_dec, d)
    return (y_prompt, y_sample,
            jnp.stack(kp_rows), jnp.stack(vp_rows), jnp.stack(ks_rows), jnp.stack(vs_rows),
            jnp.stack(sp_fin), jnp.stack(cp_fin), jnp.stack(ss_fin), jnp.stack(cs_fin))
```

```python
import functools
import math

import numpy as np
import jax
import jax.numpy as jnp
from jax import lax
from jax.experimental import pallas as pl
from jax.experimental.pallas import tpu as pltpu

F32 = jnp.float32
BF16 = jnp.bfloat16

PAGE_SIZE = 128
DA_HEAD_DIM = 64
DA_V_DIM = 128
DA_HEADS = 16
DA_SUB_HEADS = 32
DA_QK_WIDTH = 2048
ROPE_THETA = 500000.0
ROT_DIM = 16
LAMBDA_INIT_BASE = 0.8
LAMBDA_INIT_AMP = 0.6
LAMBDA_INIT_RATE = 0.3
GD_K_HEADS = 16
GD_V_HEADS = 32
GD_HEAD_DIM = 128
GD_KEY_WIDTH = 2048
GD_VAL_WIDTH = 4096
GD_CONV_CH = 8192
CONV_WIDTH = 4
CHUNK = 64
N_EXPERTS = 8
NORM_EPS = 1e-6
L2_EPS = 1e-6

LANES = 128
SUBLANES = 8
VMEM_LIMIT_BYTES = 56 * 1024 * 1024


def _params(n_grid_dims, vmem=VMEM_LIMIT_BYTES):
    return pltpu.CompilerParams(dimension_semantics=("arbitrary",) * n_grid_dims,
                                vmem_limit_bytes=vmem)


def _sigmoid(x):
    return 1.0 / (1.0 + jnp.exp(-x))


def _silu(x):
    return x * _sigmoid(x)


def _dot(a, b):
    return jnp.dot(a, b, preferred_element_type=F32)


def _dot_nt(a, b):
    return lax.dot_general(a, b, (((1,), (1,)), ((), ())), preferred_element_type=F32)


def _dot_tn(a, b):
    return lax.dot_general(a, b, (((0,), (0,)), ((), ())), preferred_element_type=F32)


def _dot_f32(a, b):
    return jnp.dot(a, b, preferred_element_type=F32, precision=lax.Precision.HIGHEST)


def _split_bf16(x):
    hi = x.astype(BF16)
    return hi, (x - hi.astype(F32)).astype(BF16)


def _rmsnorm_kernel(*refs):
    w_ref, o_ref = refs[-2:]
    x = refs[0][...]
    for extra in refs[1:-2]:
        x = x + extra[...]
    ms = jnp.mean(x * x, axis=-1, keepdims=True)
    o_ref[...] = (x * lax.rsqrt(ms + NORM_EPS) * w_ref[...]).astype(o_ref.dtype)


def _rmsnorm(xs, w, out_dtype, tm):
    m, d = xs[0].shape
    return pl.pallas_call(
        _rmsnorm_kernel,
        grid=(m // tm,),
        in_specs=[pl.BlockSpec((tm, d), lambda i: (i, 0)) for _ in xs]
                 + [pl.BlockSpec((1, d), lambda i: (0, 0))],
        out_specs=pl.BlockSpec((tm, d), lambda i: (i, 0)),
        out_shape=jax.ShapeDtypeStruct((m, d), out_dtype),
        compiler_params=_params(1),
        name="rmsnorm",
    )(*xs, w.reshape(1, d))


def _mm_kernel(*refs, n_w, n_x, n_o, epilogue, precise):
    a_ref = refs[0]
    w_refs = refs[1:1 + n_w]
    x_refs = refs[1 + n_w:1 + n_w + n_x]
    o_refs = refs[1 + n_w + n_x:1 + n_w + n_x + n_o]
    wbf_refs = refs[1 + n_w + n_x + n_o:]

    a = a_ref[...]
    if precise:
        accs = [_dot_f32(a, w_ref[...]) for w_ref in w_refs]
    else:
        @pl.when(pl.program_id(1) == 0)
        def _():
            for w_ref, wbf_ref in zip(w_refs, wbf_refs):
                wbf_ref[...] = w_ref[...].astype(BF16)

        accs = [_dot(a, wbf_ref[...]) for wbf_ref in wbf_refs]
    for o_ref, val in zip(o_refs, epilogue(accs, x_refs)):
        o_ref[...] = val.astype(o_ref.dtype)


def _matmul(a, w, widx, col_starts, n_cols, tm, tn, epilogue, extras, out_dtypes, name):
    m, k = a.shape
    precise = a.dtype == F32
    in_specs = [pl.BlockSpec((tm, k), lambda j, i: (i, 0))]
    for cs in col_starts:
        in_specs.append(pl.BlockSpec((None, k, tn), lambda j, i, cb=cs // tn: (widx, 0, cb + j)))
    for _, bs, im in extras:
        in_specs.append(pl.BlockSpec(bs, im))
    kernel = functools.partial(_mm_kernel, n_w=len(col_starts), n_x=len(extras),
                               n_o=len(out_dtypes), epilogue=epilogue, precise=precise)
    outs = pl.pallas_call(
        kernel,
        grid=(n_cols // tn, m // tm),
        in_specs=in_specs,
        out_specs=[pl.BlockSpec((tm, tn), lambda j, i: (i, j)) for _ in out_dtypes],
        out_shape=[jax.ShapeDtypeStruct((m, n_cols), dt) for dt in out_dtypes],
        scratch_shapes=[] if precise else [pltpu.VMEM((k, tn), BF16) for _ in col_starts],
        compiler_params=_params(2),
        name=name,
    )(a, *([w] * len(col_starts)), *[e[0] for e in extras])
    return outs


def _epi_plain(n_out):
    return lambda accs, xs: [accs[0]] * n_out


def _epi_residual(accs, xs):
    return [xs[0][...] + accs[0]]


def _epi_swiglu(accs, xs):
    return [_silu(accs[0]) * accs[1]]


def _epi_gated_residual(expert):
    def epi(accs, xs):
        gate = xs[1][...][:, expert:expert + 1]
        return [xs[0][...] + gate * accs[0]]
    return epi


def _apply_rope(acc, cos_ref, sin_ref):
    cos = cos_ref[...]
    sin = sin_ref[...]
    lane = lax.broadcasted_iota(jnp.int32, (1, LANES), 1) % DA_HEAD_DIM
    first_half = lane < ROT_DIM // 2
    cols = []
    for c in range(acc.shape[1] // LANES):
        x = acc[:, c * LANES:(c + 1) * LANES]
        partner = jnp.where(first_half, pltpu.roll(x, LANES - ROT_DIM // 2, 1),
                            pltpu.roll(x, ROT_DIM // 2, 1))
        cols.append(x * cos + partner * sin)
    return jnp.concatenate(cols, axis=1)


def _epi_rope(scales):
    def epi(accs, xs):
        r = _apply_rope(accs[0], xs[0], xs[1])
        return [r if s == 1.0 else r * s for s in scales]
    return epi


def _rope_table_kernel(inv_ref, sgn_ref, cos_ref, sin_ref, *, offset, period):
    rows = cos_ref.shape[0]
    t = lax.broadcasted_iota(jnp.int32, (rows, LANES), 0) % period
    pos = (t + offset).astype(F32)
    ang = pos * inv_ref[...]
    sgn = sgn_ref[...]
    cos_ref[...] = jnp.where(sgn != 0.0, jnp.cos(ang), 1.0)
    sin_ref[...] = sgn * jnp.sin(ang)


def _rope_tables(rows, offset, period):
    half = ROT_DIM // 2
    inv = ROPE_THETA ** (-2.0 * np.arange(half, dtype=np.float32) / ROT_DIM)
    lane = np.arange(LANES) % DA_HEAD_DIM
    inv_lane = np.where(lane < ROT_DIM, inv[lane % half], 0.0).astype(np.float32)
    sgn_lane = np.where(lane < half, -1.0, np.where(lane < ROT_DIM, 1.0, 0.0)).astype(np.float32)
    kernel = functools.partial(_rope_table_kernel, offset=offset, period=period)
    return pl.pallas_call(
        kernel,
        out_shape=[jax.ShapeDtypeStruct((rows, LANES), F32)] * 2,
        name="rope_tables",
    )(jnp.asarray(inv_lane).reshape(1, LANES), jnp.asarray(sgn_lane).reshape(1, LANES))


def _diff_lambda(lam_ref, lam_init):
    lv = lam_ref[...]
    a = jnp.sum(lv[0:1] * lv[1:2], axis=-1, keepdims=True)
    b = jnp.sum(lv[2:3] * lv[3:4], axis=-1, keepdims=True)
    return jnp.exp(a) - jnp.exp(b) + lam_init


def _sub_layer_norm(o, subln_ref, lam_init):
    ms = jnp.mean(o * o, axis=-1, keepdims=True)
    return o * lax.rsqrt(ms + NORM_EPS) * subln_ref[...] * (1.0 - lam_init)


def _attn_prompt_kernel(lam_ref, subln_ref, q_ref, k_ref, v_ref, o_ref, m_sc, l_sc, acc_sc,
                        *, tq, tk, lam_init, heads):
    qi = pl.program_id(2)
    ki = pl.program_id(3)

    @pl.when(ki == 0)
    def _():
        m_sc[...] = jnp.full(m_sc.shape, -jnp.inf, F32)
        l_sc[...] = jnp.zeros(l_sc.shape, F32)
        acc_sc[...] = jnp.zeros(acc_sc.shape, F32)

    def accumulate(masked):
        lane = lax.broadcasted_iota(jnp.int32, (1, LANES), 1)
        if masked:
            row = lax.broadcasted_iota(jnp.int32, (tq, tk), 0)
            col = lax.broadcasted_iota(jnp.int32, (tq, tk), 1)
            causal = col <= row
        for hh in range(heads):
            cols = slice(hh * LANES, (hh + 1) * LANES)
            q = q_ref[:, cols]
            k = k_ref[:, cols]
            v = v_ref[:, cols]
            zero = jnp.zeros_like(q)
            for sub in range(2):
                idx = 2 * hh + sub
                in_sub = (lane < DA_HEAD_DIM) if sub == 0 else (lane >= DA_HEAD_DIM)
                s = _dot_nt(jnp.where(in_sub, q, zero), k)
                if masked:
                    s = jnp.where(causal, s, -jnp.inf)
                m_prev = m_sc[idx]
                m_new = jnp.maximum(m_prev, jnp.max(s, axis=-1, keepdims=True))
                alpha = jnp.exp(m_prev - m_new)
                p = jnp.exp(s - m_new[:, :1])
                l_sc[idx] = alpha * l_sc[idx] + jnp.sum(p, axis=-1, keepdims=True)
                acc_sc[idx] = alpha * acc_sc[idx] + _dot(p.astype(BF16), v)
                m_sc[idx] = m_new

    @pl.when(ki < qi)
    def _():
        accumulate(False)

    @pl.when(ki == qi)
    def _():
        accumulate(True)
        lam = _diff_lambda(lam_ref, lam_init)
        for hh in range(heads):
            o = acc_sc[2 * hh] / l_sc[2 * hh] - lam * (acc_sc[2 * hh + 1] / l_sc[2 * hh + 1])
            o_ref[:, hh * LANES:(hh + 1) * LANES] = _sub_layer_norm(o, subln_ref, lam_init).astype(o_ref.dtype)


def _attn_prompt(q, k, v, lam_vecs, subln, batch, seq, lam_init, tq=512, heads=4):
    nq = seq // tq
    width = heads * LANES
    kernel = functools.partial(_attn_prompt_kernel, tq=tq, tk=tq, lam_init=lam_init, heads=heads)
    kv_spec = pl.BlockSpec((tq, width), lambda b, h, qi, ki: (b * nq + jnp.minimum(ki, qi), h))
    return pl.pallas_call(
        kernel,
        grid=(batch, DA_HEADS // heads, nq, nq),
        in_specs=[pl.BlockSpec((SUBLANES, LANES), lambda b, h, qi, ki: (0, 0)),
                  pl.BlockSpec((1, LANES), lambda b, h, qi, ki: (0, 0)),
                  pl.BlockSpec((tq, width), lambda b, h, qi, ki: (b * nq + qi, h)),
                  kv_spec, kv_spec],
        out_specs=pl.BlockSpec((tq, width), lambda b, h, qi, ki: (b * nq + qi, h)),
        out_shape=jax.ShapeDtypeStruct((batch * seq, DA_HEADS * DA_V_DIM), BF16),
        scratch_shapes=[pltpu.VMEM((2 * heads, tq, LANES), F32) for _ in range(3)],
        compiler_params=_params(4),
        name="attn_prompt",
    )(lam_vecs, subln, q, k, v)


def _attn_decode_kernel(pt_ref, lam_ref, subln_ref, spread_ref, qrep_ref, q_ref, kn_ref, vn_ref, *rest,
                        lam_init, n_blk):
    del pt_ref
    k_refs = rest[:n_blk]
    v_refs = rest[n_blk:2 * n_blk]
    o_ref, m_sc, l_sc, acc_sc = rest[2 * n_blk:]
    p = pl.program_id(1)

    @pl.when(p == 0)
    def _():
        m_sc[...] = jnp.full(m_sc.shape, -jnp.inf, F32)
        l_sc[...] = jnp.zeros(l_sc.shape, F32)
        acc_sc[...] = jnp.zeros(acc_sc.shape, F32)

    def update(s, weighted_values):
        m_prev = m_sc[...]
        m_new = jnp.maximum(m_prev, jnp.max(s, axis=-1, keepdims=True))
        alpha = jnp.exp(m_prev - m_new)
        pe = jnp.exp(s - m_new[:, :1])
        l_sc[...] = alpha * l_sc[...] + jnp.sum(pe, axis=-1, keepdims=True)
        acc_sc[...] = alpha * acc_sc[...] + weighted_values(pe)
        m_sc[...] = m_new

    width = PAGE_SIZE * DA_HEADS
    row_head = (lax.broadcasted_iota(jnp.int32, (2 * DA_SUB_HEADS, width), 0) % DA_SUB_HEADS) // 2
    col_head = lax.broadcasted_iota(jnp.int32, (2 * DA_SUB_HEADS, width), 1) % DA_HEADS
    own_head = row_head == col_head
    q_rep = qrep_ref[...]
    for k_ref, v_ref in zip(k_refs, v_refs):
        s = jnp.sum(k_ref[...] * q_rep, axis=1)

        def page_values(pe, v_ref=v_ref):
            p_hi, p_lo = _split_bf16(pe)
            spread = _dot(jnp.concatenate([p_hi, p_lo], axis=0), spread_ref[...])
            pm = jnp.where(own_head, spread, 0.0).astype(BF16)
            v_hi, v_lo = _split_bf16(v_ref[...].reshape(width, DA_V_DIM))
            both = _dot(pm, v_hi)
            return both[:DA_SUB_HEADS] + both[DA_SUB_HEADS:] + _dot(pm[:DA_SUB_HEADS], v_lo)

        update(s, page_values)

    @pl.when(p == pl.num_programs(1) - 1)
    def _():
        s_new = jnp.sum(q_ref[...] * kn_ref[...], axis=-1, keepdims=True)
        update(s_new, lambda pe: pe * vn_ref[...])
        lam = _diff_lambda(lam_ref, lam_init)
        acc_sc[...] = acc_sc[...] / l_sc[...]
        o = (acc_sc[pl.ds(0, DA_HEADS, stride=2), :]
             - lam * acc_sc[pl.ds(1, DA_HEADS, stride=2), :])
        o_ref[...] = _sub_layer_norm(o, subln_ref, lam_init).astype(o_ref.dtype)


def _attn_decode(q, cache_k, cache_v, layer, page_table, k_new, v_new, lam_vecs, subln, lam_init,
                 pages_per_step=4):
    bd, n_pages = page_table.shape
    n_blk = pages_per_step
    assert n_pages % n_blk == 0
    keys_t = jnp.transpose(cache_k, (0, 1, 3, 4, 2))
    q_rep = jnp.broadcast_to(q[..., None], q.shape + (PAGE_SIZE,))
    v_new2 = jnp.repeat(v_new, 2, axis=1)
    width = PAGE_SIZE * DA_HEADS
    spread = jnp.asarray(np.arange(width)[None, :] // DA_HEADS == np.arange(PAGE_SIZE)[:, None], BF16)
    kernel = functools.partial(_attn_decode_kernel, lam_init=lam_init, n_blk=n_blk)
    per_seq = lambda *dims: pl.BlockSpec((None,) + dims, lambda b, p, pt: (b,) + (0,) * len(dims))
    page = lambda dims, i: pl.BlockSpec(
        (None, None) + dims, lambda b, p, pt: (layer, pt[b, p * n_blk + i], 0, 0, 0))
    grid_spec = pltpu.PrefetchScalarGridSpec(
        num_scalar_prefetch=1,
        grid=(bd, n_pages // n_blk),
        in_specs=[
            pl.BlockSpec((SUBLANES, LANES), lambda b, p, pt: (0, 0)),
            pl.BlockSpec((1, LANES), lambda b, p, pt: (0, 0)),
            pl.BlockSpec((PAGE_SIZE, width), lambda b, p, pt: (0, 0)),
            per_seq(DA_SUB_HEADS, DA_HEAD_DIM, PAGE_SIZE),
            per_seq(DA_SUB_HEADS, DA_HEAD_DIM),
            per_seq(DA_SUB_HEADS, DA_HEAD_DIM),
            per_seq(DA_SUB_HEADS, DA_V_DIM),
        ] + [page((DA_SUB_HEADS, DA_HEAD_DIM, PAGE_SIZE), i) for i in range(n_blk)]
          + [page((PAGE_SIZE, DA_HEADS, DA_V_DIM), i) for i in range(n_blk)],
        out_specs=per_seq(DA_HEADS, DA_V_DIM),
        scratch_shapes=[pltpu.VMEM((DA_SUB_HEADS, LANES), F32), pltpu.VMEM((DA_SUB_HEADS, LANES), F32),
                        pltpu.VMEM((DA_SUB_HEADS, DA_V_DIM), F32)],
    )
    return pl.pallas_call(
        kernel,
        grid_spec=grid_spec,
        out_shape=jax.ShapeDtypeStruct((bd, DA_HEADS, DA_V_DIM), F32),
        compiler_params=_params(2),
        name="attn_decode",
    )(page_table, lam_vecs, subln, spread, q_rep, q, k_new, v_new2,
      *([keys_t] * n_blk), *([cache_v] * n_blk))


def _gdn_gates_kernel(ba_ref, alog_ref, dtb_ref, o_ref):
    x = ba_ref[...]
    lane = lax.broadcasted_iota(jnp.int32, x.shape, 1)
    beta = _sigmoid(x)
    y = x + dtb_ref[...]
    softplus = jnp.maximum(y, 0.0) + jnp.log1p(jnp.exp(-jnp.abs(y)))
    g = -jnp.exp(alog_ref[...]) * softplus
    o_ref[...] = jnp.where(lane < GD_V_HEADS, beta, jnp.where(lane < 2 * GD_V_HEADS, g, 0.0))


def _gdn_gates(ba, a_log, dt_bias, tm):
    m = ba.shape[0]
    pad = lambda v: jnp.pad(v.reshape(1, GD_V_HEADS), ((0, 0), (GD_V_HEADS, LANES - 2 * GD_V_HEADS)))
    return pl.pallas_call(
        _gdn_gates_kernel,
        grid=(m // tm,),
        in_specs=[pl.BlockSpec((tm, LANES), lambda i: (i, 0)),
                  pl.BlockSpec((1, LANES), lambda i: (0, 0)),
                  pl.BlockSpec((1, LANES), lambda i: (0, 0))],
        out_specs=pl.BlockSpec((tm, LANES), lambda i: (i, 0)),
        out_shape=jax.ShapeDtypeStruct((m, LANES), F32),
        compiler_params=_params(1),
        name="gdn_gates",
    )(ba, pad(a_log), pad(dt_bias))


def _gdn_conv_kernel(x_ref, buf_ref, w_ref, o_ref, xin_sc, *, heads):
    t = x_ref.shape[0]
    xin_sc[0:SUBLANES, :] = buf_ref[...]
    xin_sc[SUBLANES:SUBLANES + t, :] = x_ref[...]
    for hh in range(heads):
        c = pl.program_id(1) * heads + hh
        cols = slice(hh * LANES, (hh + 1) * LANES)
        w = w_ref[:, cols]
        y = x_ref[:, cols] * w[CONV_WIDTH - 1:CONV_WIDTH]
        for s in range(1, CONV_WIDTH):
            shifted = xin_sc[pl.ds(SUBLANES - s, t), cols]
            y = y + shifted * w[CONV_WIDTH - 1 - s:CONV_WIDTH - s]
        y = _silu(y)
        inv = lax.rsqrt(jnp.sum(y * y, axis=-1, keepdims=True) + L2_EPS)
        scale = jnp.where(c < GD_K_HEADS, GD_HEAD_DIM ** -0.5, 1.0)
        o_ref[:, cols] = jnp.where(c < 2 * GD_K_HEADS, y * (inv * scale), y)


def _gdn_conv(x, buf, conv_w, widx, batch, t, heads):
    width = heads * LANES
    return pl.pallas_call(
        functools.partial(_gdn_conv_kernel, heads=heads),
        grid=(batch, GD_CONV_CH // width),
        in_specs=[pl.BlockSpec((t, width), lambda b, c: (b, c)),
                  pl.BlockSpec((SUBLANES, width), lambda b, c: (b, c)),
                  pl.BlockSpec((None, CONV_WIDTH, width), lambda b, c: (widx, 0, c))],
        out_specs=pl.BlockSpec((t, width), lambda b, c: (b, c)),
        out_shape=jax.ShapeDtypeStruct((batch * t, GD_CONV_CH), F32),
        scratch_shapes=[pltpu.VMEM((SUBLANES + t, width), F32)],
        compiler_params=_params(2),
        name="gdn_conv",
    )(x, buf, conv_w)


def _split_products(lhs_list, rhs_list):
    ls = [_split_bf16(a) for a in lhs_list]
    rs = [_split_bf16(b) for b in rhs_list]
    first = [_dot(jnp.concatenate([a_hi, a_lo], axis=0), b_hi) for (a_hi, a_lo), (b_hi, _) in zip(ls, rs)]
    second = [_dot(a_hi, b_lo) for (a_hi, _), (_, b_lo) in zip(ls, rs)]
    outs = []
    for f, s, a in zip(first, second, lhs_list):
        n = a.shape[0]
        outs.append(f[:n] + f[n:] + s)
    return outs


def _unit_lower_inverses(ms, eye):
    n = eye.shape[0]
    invs = [eye - m for m in ms]
    powers = _split_products(ms, ms)
    n_factors = int(math.log2(CHUNK)) - 1
    for f in range(n_factors):
        last = f + 1 == n_factors
        lhs = invs if last else [jnp.concatenate([i, p], axis=0) for i, p in zip(invs, powers)]
        prods = _split_products(lhs, powers)
        invs = [i + pr[:n] for i, pr in zip(invs, prods)]
        if not last:
            powers = [pr[n:] for pr in prods]
    return invs


def _gdn_chunk_kernel(q_ref, k_ref, v_ref, z_ref, bg_ref, gt_ref, gn_ref, s0_ref, o_ref, sfin_ref,
                      s_sc, *, heads):
    grp = pl.program_id(1)
    c = pl.program_id(2)
    pairs = heads // 2
    two = 2 * CHUNK

    @pl.when(c == 0)
    def _():
        s_sc[...] = s0_ref[...].reshape(s_sc.shape)

    bg = bg_ref[...]
    gn = gn_ref[...]
    lane = lax.broadcasted_iota(jnp.int32, (1, LANES), 1)
    ri = lax.broadcasted_iota(jnp.int32, (two, two), 0)
    ci = lax.broadcasted_iota(jnp.int32, (two, two), 1)
    same = (ri // CHUNK) == (ci // CHUNK)
    tril = same & (ri >= ci)
    triu = same & (ri <= ci)
    strict = same & (ri > ci)
    eye = (ri == ci).astype(F32)
    top_rows = lax.broadcasted_iota(jnp.int32, (two, 1), 0) < CHUNK
    top_state = lax.broadcasted_iota(jnp.int32, (2 * GD_HEAD_DIM, 1), 0) < GD_HEAD_DIM

    def stack2(x):
        return jnp.concatenate([x, x], axis=0)

    def blocks(x):
        zero = jnp.zeros_like(x)
        return jnp.concatenate([jnp.where(top_rows, x, zero), jnp.where(top_rows, zero, x)], axis=1)

    def column(lane_idx):
        return jnp.sum(jnp.where(lane == lane_idx, bg, 0.0), axis=-1, keepdims=True)

    qs = [stack2(q_ref[:, kh * LANES:(kh + 1) * LANES]) for kh in range(pairs)]
    ks = [stack2(k_ref[:, kh * LANES:(kh + 1) * LANES]) for kh in range(pairs)]
    vs = [jnp.concatenate([v_ref[:, (2 * kh) * LANES:(2 * kh + 1) * LANES],
                           v_ref[:, (2 * kh + 1) * LANES:(2 * kh + 2) * LANES]], axis=0)
          for kh in range(pairs)]
    kq = [_dot_nt(jnp.concatenate([k, q], axis=0).astype(BF16), k.astype(BF16)) for k, q in zip(ks, qs)]

    betas, gcums, decays, egs = [], [], [], []
    for kh in range(pairs):
        h = grp * heads + 2 * kh
        beta = jnp.concatenate([column(h), column(h + 1)], axis=0)
        g_col = jnp.concatenate([column(h + GD_V_HEADS), column(h + 1 + GD_V_HEADS)], axis=0)
        g_row = gt_ref[kh:kh + 1, :]
        gcum_col = jnp.sum(jnp.where(tril, g_row, 0.0), axis=-1, keepdims=True)
        gcum_row = jnp.sum(jnp.where(triu, g_col, 0.0), axis=0, keepdims=True)
        betas.append(beta)
        gcums.append(gcum_col)
        decays.append(jnp.exp(jnp.where(tril, gcum_col - gcum_row, -jnp.inf)))
        egs.append(jnp.exp(gcum_col))

    tinvs = _unit_lower_inverses(
        [jnp.where(strict, b * x[:two] * d, 0.0) for b, x, d in zip(betas, kq, decays)], eye)
    rhs = [jnp.concatenate([v * b, k * (b * e)], axis=1).astype(BF16)
           for v, k, b, e in zip(vs, ks, betas, egs)]
    wus = [_dot(t.astype(BF16), r) for t, r in zip(tinvs, rhs)]
    s_olds = [s_sc[kh] for kh in range(pairs)]
    s_bfs = [s.astype(BF16) for s in s_olds]
    v_news = [wu[:, :LANES] - _dot(blocks(wu[:, LANES:]).astype(BF16), s)
              for wu, s in zip(wus, s_bfs)]
    v_bfs = [v.astype(BF16) for v in v_news]
    outs = [_dot(jnp.concatenate([blocks(q * e), x[two:] * d], axis=1).astype(BF16),
                 jnp.concatenate([s, v], axis=0))
            for q, e, x, d, s, v in zip(qs, egs, kq, decays, s_bfs, v_bfs)]
    g_lasts = [jnp.where(top_rows, g[CHUNK - 1:CHUNK, :], g[two - 1:two, :]) for g in gcums]
    updates = [_dot_tn(blocks(k * jnp.exp(gl - g)).astype(BF16), v)
               for k, gl, g, v in zip(ks, g_lasts, gcums, v_bfs)]
    for kh in range(pairs):
        g = gcums[kh]
        carry = jnp.exp(jnp.where(top_state, g[CHUNK - 1:CHUNK, :], g[two - 1:two, :]))
        s_sc[kh] = s_olds[kh] * carry + updates[kh]
        o = outs[kh]
        z = jnp.concatenate([z_ref[:, (2 * kh) * LANES:(2 * kh + 1) * LANES],
                             z_ref[:, (2 * kh + 1) * LANES:(2 * kh + 2) * LANES]], axis=0)
        ms = jnp.mean(o * o, axis=-1, keepdims=True)
        on = (o * lax.rsqrt(ms + NORM_EPS) * gn * _silu(z)).astype(o_ref.dtype)
        o_ref[:, (2 * kh) * LANES:(2 * kh + 1) * LANES] = on[:CHUNK]
        o_ref[:, (2 * kh + 1) * LANES:(2 * kh + 2) * LANES] = on[CHUNK:]

    @pl.when(c == pl.num_programs(2) - 1)
    def _():
        sfin_ref[...] = s_sc[...].reshape(sfin_ref.shape)


def _gdn_chunks(qkv, proj, bg, g_t, gnorm, s0, batch, t, heads=16):
    nc = t // CHUNK
    n_grp = GD_V_HEADS // heads
    pairs = heads // 2
    kw = pairs * LANES
    vw = heads * LANES
    row = lambda b, g, c: b * nc + c
    kernel = functools.partial(_gdn_chunk_kernel, heads=heads)
    return pl.pallas_call(
        kernel,
        grid=(batch, n_grp, nc),
        in_specs=[
            pl.BlockSpec((CHUNK, kw), lambda b, g, c: (row(b, g, c), g)),
            pl.BlockSpec((CHUNK, kw), lambda b, g, c: (row(b, g, c), GD_KEY_WIDTH // kw + g)),
            pl.BlockSpec((CHUNK, vw), lambda b, g, c: (row(b, g, c), 2 * GD_KEY_WIDTH // vw + g)),
            pl.BlockSpec((CHUNK, vw), lambda b, g, c: (row(b, g, c), GD_CONV_CH // vw + g)),
            pl.BlockSpec((CHUNK, LANES), lambda b, g, c: (row(b, g, c), 0)),
            pl.BlockSpec((None, None, pairs, 2 * CHUNK), lambda b, g, c: (b, c, g, 0)),
            pl.BlockSpec((1, LANES), lambda b, g, c: (0, 0)),
            pl.BlockSpec((None, heads, GD_HEAD_DIM, GD_HEAD_DIM), lambda b, g, c: (b, g, 0, 0)),
        ],
        out_specs=[
            pl.BlockSpec((CHUNK, vw), lambda b, g, c: (row(b, g, c), g)),
            pl.BlockSpec((None, heads, GD_HEAD_DIM, GD_HEAD_DIM), lambda b, g, c: (b, g, 0, 0)),
        ],
        out_shape=[jax.ShapeDtypeStruct((batch * t, GD_VAL_WIDTH), BF16),
                   jax.ShapeDtypeStruct((batch, GD_V_HEADS, GD_HEAD_DIM, GD_HEAD_DIM), F32)],
        scratch_shapes=[pltpu.VMEM((pairs, 2 * GD_HEAD_DIM, GD_HEAD_DIM), F32)],
        compiler_params=_params(3),
        name="gdn_chunks",
    )(qkv, qkv, qkv, proj, bg, g_t, gnorm, s0)


def _gdn_step_kernel(qkv_ref, z_ref, bg_ref, gn_ref, s0_ref, o_ref, sfin_ref):
    ri = lax.broadcasted_iota(jnp.int32, (GD_HEAD_DIM, GD_HEAD_DIM), 0)
    ci = lax.broadcasted_iota(jnp.int32, (GD_HEAD_DIM, GD_HEAD_DIM), 1)
    diag = ri == ci
    bg = bg_ref[...]
    gn = gn_ref[...]

    def column(row_vec):
        return jnp.sum(jnp.where(diag, row_vec, 0.0), axis=-1, keepdims=True)

    for kh in range(GD_K_HEADS):
        q_col = column(qkv_ref[:, kh * LANES:(kh + 1) * LANES])
        k_col = column(qkv_ref[:, GD_KEY_WIDTH + kh * LANES:GD_KEY_WIDTH + (kh + 1) * LANES])
        for hh in (2 * kh, 2 * kh + 1):
            v = qkv_ref[:, 2 * GD_KEY_WIDTH + hh * LANES:2 * GD_KEY_WIDTH + (hh + 1) * LANES]
            beta = bg[:, hh:hh + 1]
            eg = jnp.exp(bg[:, GD_V_HEADS + hh:GD_V_HEADS + hh + 1])
            s_old = s0_ref[hh]
            sk = jnp.sum(s_old * k_col, axis=0, keepdims=True)
            v_new = beta * (v - eg * sk)
            s_new = eg * s_old + k_col * v_new
            sfin_ref[hh] = s_new
            o = jnp.sum(s_new * q_col, axis=0, keepdims=True)
            z = z_ref[:, hh * LANES:(hh + 1) * LANES]
            ms = jnp.mean(o * o, axis=-1, keepdims=True)
            o_ref[:, hh * LANES:(hh + 1) * LANES] = o * lax.rsqrt(ms + NORM_EPS) * gn * _silu(z)


def _gdn_step(qkv, z, bg, gnorm, s0):
    batch = qkv.shape[0]
    row = lambda width: pl.BlockSpec((None, 1, width), lambda b: (b, 0, 0))
    state = pl.BlockSpec((None, GD_V_HEADS, GD_HEAD_DIM, GD_HEAD_DIM), lambda b: (b, 0, 0, 0))
    return pl.pallas_call(
        _gdn_step_kernel,
        grid=(batch,),
        in_specs=[row(GD_CONV_CH), row(GD_VAL_WIDTH), row(LANES),
                  pl.BlockSpec((1, LANES), lambda b: (0, 0)), state],
        out_specs=[row(GD_VAL_WIDTH), state],
        out_shape=[jax.ShapeDtypeStruct((batch, 1, GD_VAL_WIDTH), F32),
                   jax.ShapeDtypeStruct(s0.shape, F32)],
        compiler_params=_params(1),
        name="gdn_step",
    )(qkv, z, bg, gnorm, s0)


def _router_kernel(h_ref, w_ref, o_ref):
    if h_ref.dtype == F32:
        logits = _dot_f32(h_ref[...], w_ref[...])
    else:
        logits = _dot(h_ref[...], w_ref[...].astype(BF16))
    e = jnp.exp(logits - jnp.max(logits, axis=-1, keepdims=True))
    probs = e / jnp.sum(e, axis=-1, keepdims=True)
    lane = lax.broadcasted_iota(jnp.int32, probs.shape, 1)
    v1 = jnp.max(probs, axis=-1, keepdims=True)
    i1 = jnp.min(jnp.where(probs == v1, lane, N_EXPERTS), axis=-1, keepdims=True)
    rest = jnp.where(lane == i1, -1.0, probs)
    v2 = jnp.max(rest, axis=-1, keepdims=True)
    i2 = jnp.min(jnp.where(rest == v2, lane, N_EXPERTS), axis=-1, keepdims=True)
    total = v1 + v2
    o_ref[...] = jnp.where(lane == i1, v1 / total, 0.0) + jnp.where(lane == i2, v2 / total, 0.0)


def _router(h, w_router, widx, tm):
    m, d = h.shape
    return pl.pallas_call(
        _router_kernel,
        grid=(m // tm,),
        in_specs=[pl.BlockSpec((tm, d), lambda i: (i, 0)),
                  pl.BlockSpec((None, d, N_EXPERTS), lambda i: (widx, 0, 0))],
        out_specs=pl.BlockSpec((tm, N_EXPERTS), lambda i: (i, 0)),
        out_shape=jax.ShapeDtypeStruct((m, N_EXPERTS), F32),
        compiler_params=_params(1),
        name="moe_router",
    )(h, w_router)


MOE_BLOCK = 1024
MOE_SUB = 128
MOE_FF_CHUNK = 256


def _moe_kernel(cnt_ref, h_ref, gc_ref, gr_ref, wg_ref, wu_ref, wd_ref, o_ref,
                xg_sc, y_sc, rc_sc, rr_sc, gate_sc):
    i = pl.program_id(0)
    e = pl.program_id(1)
    f = pl.program_id(2)
    tb = h_ref.shape[0]
    n_sub = tb // MOE_SUB
    cnt = cnt_ref[i, e]
    rows = lambda s: slice(s * MOE_SUB, (s + 1) * MOE_SUB)

    @pl.when((e == 0) & (f == 0))
    def _():
        o_ref[...] = jnp.zeros(o_ref.shape, F32)
        ri = lax.broadcasted_iota(jnp.int32, (MOE_SUB, MOE_SUB), 0)
        ci = lax.broadcasted_iota(jnp.int32, (MOE_SUB, MOE_SUB), 1)
        before_c = jnp.where(ci < ri, 1.0, 0.0).astype(BF16)
        before_r = jnp.where(ri < ci, 1.0, 0.0).astype(BF16)
        off_c = jnp.zeros((1, N_EXPERTS), F32)
        off_r = jnp.zeros((N_EXPERTS, 1), F32)
        for s in range(n_sub):
            sel_c = jnp.where(gc_ref[rows(s), :] > 0.0, 1.0, 0.0)
            rc_sc[rows(s), :] = _dot(before_c, sel_c.astype(BF16)) + off_c
            off_c = off_c + jnp.sum(sel_c, axis=0, keepdims=True)
            sel_r = jnp.where(gr_ref[:, rows(s)] > 0.0, 1.0, 0.0)
            rr_sc[:, rows(s)] = _dot(sel_r.astype(BF16), before_r) + off_r
            off_r = off_r + jnp.sum(sel_r, axis=1, keepdims=True)

    @pl.when(f == 0)
    def _():
        rank_r = rr_sc[pl.ds(e, 1), :]
        gate_r = gr_ref[pl.ds(e, 1), :]
        for s in range(n_sub):
            @pl.when(s * MOE_SUB < cnt)
            def _():
                slot = (lax.broadcasted_iota(jnp.int32, (MOE_SUB, 1), 0) + s * MOE_SUB).astype(F32)
                pick = (rank_r == slot) & (gate_r > 0.0)
                xg_sc[rows(s), :] = _dot(jnp.where(pick, 1.0, 0.0).astype(BF16),
                                         h_ref[...]).astype(BF16)
                gate_sc[rows(s), :] = jnp.sum(jnp.where(pick, gate_r, 0.0), axis=1, keepdims=True)
                y_sc[rows(s), :] = jnp.zeros((MOE_SUB, y_sc.shape[1]), F32)

    for s in range(n_sub):
        @pl.when(s * MOE_SUB < cnt)
        def _():
            xs = xg_sc[rows(s), :]
            hid = _silu(_dot(xs, wg_ref[...])) * _dot(xs, wu_ref[...])
            y_sc[rows(s), :] += _dot(hid.astype(BF16), wd_ref[...])

    @pl.when(f == pl.num_programs(2) - 1)
    def _():
        lane = lax.broadcasted_iota(jnp.int32, (1, N_EXPERTS), 1)
        rank_c = jnp.sum(jnp.where(lane == e, rc_sc[...], 0.0), axis=1, keepdims=True)
        gate_c = jnp.sum(jnp.where(lane == e, gc_ref[...], 0.0), axis=1, keepdims=True)
        for s in range(n_sub):
            @pl.when(s * MOE_SUB < cnt)
            def _():
                y_hi, y_lo = _split_bf16(y_sc[rows(s), :] * gate_sc[rows(s), :])
                slot = (lax.broadcasted_iota(jnp.int32, (1, MOE_SUB), 1) + s * MOE_SUB).astype(F32)
                place = jnp.where((rank_c == slot) & (gate_c > 0.0), 1.0, 0.0).astype(BF16)
                o_ref[...] += _dot(jnp.concatenate([place, place], axis=1),
                                   jnp.concatenate([y_hi, y_lo], axis=0))


def _moe_routed(h, gates, w_gu, w_down):
    m, d = h.shape
    n_exp, f_dim, _ = w_down.shape
    nb = m // MOE_BLOCK
    nf = f_dim // MOE_FF_CHUNK
    counts = jnp.sum((gates > 0.0).reshape(nb, MOE_BLOCK, n_exp), axis=1).astype(jnp.int32)
    w_gu_bf = w_gu.astype(BF16).reshape(n_exp, d, 2 * nf, MOE_FF_CHUNK).transpose(0, 2, 1, 3)
    w_down_bf = w_down.astype(BF16)
    grid_spec = pltpu.PrefetchScalarGridSpec(
        num_scalar_prefetch=1,
        grid=(nb, n_exp, nf),
        in_specs=[
            pl.BlockSpec((MOE_BLOCK, d), lambda i, e, f, c: (i, 0)),
            pl.BlockSpec((MOE_BLOCK, n_exp), lambda i, e, f, c: (i, 0)),
            pl.BlockSpec((n_exp, MOE_BLOCK), lambda i, e, f, c: (0, i)),
            pl.BlockSpec((None, None, d, MOE_FF_CHUNK), lambda i, e, f, c: (e, f, 0, 0)),
            pl.BlockSpec((None, None, d, MOE_FF_CHUNK), lambda i, e, f, c: (e, nf + f, 0, 0)),
            pl.BlockSpec((None, MOE_FF_CHUNK, d), lambda i, e, f, c: (e, f, 0)),
        ],
        out_specs=pl.BlockSpec((MOE_BLOCK, d), lambda i, e, f, c: (i, 0)),
        scratch_shapes=[pltpu.VMEM((MOE_BLOCK, d), BF16), pltpu.VMEM((MOE_BLOCK, d), F32),
                        pltpu.VMEM((MOE_BLOCK, n_exp), F32), pltpu.VMEM((n_exp, MOE_BLOCK), F32),
                        pltpu.VMEM((MOE_BLOCK, 1), F32)],
    )
    return pl.pallas_call(
        _moe_kernel,
        grid_spec=grid_spec,
        out_shape=jax.ShapeDtypeStruct((m, d), F32),
        compiler_params=_params(3),
        name="moe_routed",
    )(counts, h, gates, gates.T, w_gu_bf, w_gu_bf, w_down_bf)


def _lambda_vectors(lq1, lk1, lq2, lk2):
    rows = jnp.stack([lq1, lk1, lq2, lk2]).astype(F32)
    return jnp.pad(rows, ((0, SUBLANES - 4), (0, LANES - DA_HEAD_DIM)))


def _attention_layer(x, tm, layer, j, w, rope, attend, decode):
    d = x.shape[1]
    act = F32 if decode else BF16
    lam_init = LAMBDA_INIT_BASE - LAMBDA_INIT_AMP * math.exp(-LAMBDA_INIT_RATE * layer)
    lam_vecs = _lambda_vectors(w['lambda_q1'][j], w['lambda_k1'][j], w['lambda_q2'][j], w['lambda_k2'][j])
    subln = w['subln_da'][j].reshape(1, DA_V_DIM)
    cos, sin, n_tab = rope
    tabs = [(arr, (tm, LANES), lambda jj, i: (i % n_tab, 0)) for arr in (cos, sin)]
    h = _rmsnorm([x], w['norm_mix'][layer], act, tm)
    tn = 512
    kv_dtypes = [F32] if decode else [F32, BF16]
    (q,) = _matmul(h, w['w_qkv_da'], j, [0], DA_QK_WIDTH, tm, tn,
                   _epi_rope([DA_HEAD_DIM ** -0.5]), tabs, [act], "da_q")
    k = _matmul(h, w['w_qkv_da'], j, [DA_QK_WIDTH], DA_QK_WIDTH, tm, tn,
                _epi_rope([1.0] * len(kv_dtypes)), tabs, kv_dtypes, "da_k")
    v = _matmul(h, w['w_qkv_da'], j, [2 * DA_QK_WIDTH], DA_HEADS * DA_V_DIM, tm, tn,
                _epi_plain(len(kv_dtypes)), [], kv_dtypes, "da_v")
    o = attend(q, k[-1], v[-1], lam_vecs, subln, lam_init)
    res = (x, (tm, tn), lambda jj, i: (i, jj))
    (x,) = _matmul(o, w['w_o_da'], j, [0], d, tm, tn, _epi_residual, [res], [F32], "da_out")
    h = _rmsnorm([x], w['norm_ffn'][layer], act, tm)
    d_ff = w['w_down_dense'].shape[1]
    (hid,) = _matmul(h, w['w_gu_dense'], j, [0, d_ff], d_ff, tm, tn, _epi_swiglu, [], [act], "ffn_gu")
    tm_down = min(tm, 256)
    res = (x, (tm_down, tn), lambda jj, i: (i, jj))
    (x,) = _matmul(hid, w['w_down_dense'], j, [0], d, tm_down, tn, _epi_residual, [res], [F32], "ffn_down")
    return x, k[0], v[0]


def _deltanet_layer(x, tm, layer, j, w, batch, t, conv_buf, s0, decode):
    d = x.shape[1]
    act = F32 if decode else BF16
    h = _rmsnorm([x], w['norm_mix'][layer], act, tm)
    n_main = GD_CONV_CH + GD_VAL_WIDTH
    (proj,) = _matmul(h, w['w_in_gd'], j, [0], n_main, tm, 512, _epi_plain(1), [], [F32], "gd_in")
    (ba,) = _matmul(h, w['w_in_gd'], j, [n_main], LANES, tm, LANES, _epi_plain(1), [], [F32], "gd_in_ba")
    bg = _gdn_gates(ba, w['a_log_gd'][j], w['dt_bias_gd'][j], tm)
    gnorm = w['gnorm_gd'][j].reshape(1, GD_HEAD_DIM)
    if decode:
        conv_in = jnp.pad(proj.reshape(batch, 1, -1), ((0, 0), (0, SUBLANES - 1), (0, 0)))
        qkv = _gdn_conv(conv_in.reshape(batch * SUBLANES, -1), conv_buf, w['conv_w_gd'], j, batch,
                        SUBLANES, GD_K_HEADS)
        qkv = qkv.reshape(batch, SUBLANES, -1)[:, :1]
        z = proj[:, GD_CONV_CH:n_main].reshape(batch, 1, GD_VAL_WIDTH)
        o, s_fin = _gdn_step(qkv, z, bg.reshape(batch, 1, LANES), gnorm, s0)
        o = o.reshape(batch, GD_VAL_WIDTH)
    else:
        qkv = _gdn_conv(proj, conv_buf, w['conv_w_gd'], j, batch, t, 4)
        g_t = bg[:, GD_V_HEADS:2 * GD_V_HEADS].reshape(batch, t // CHUNK, CHUNK, GD_K_HEADS, 2)
        g_t = g_t.transpose(0, 1, 3, 4, 2).reshape(batch, t // CHUNK, GD_K_HEADS, 2 * CHUNK)
        o, s_fin = _gdn_chunks(qkv, proj, bg, g_t, gnorm, s0, batch, t)
    res = (x, (tm, 512), lambda jj, i: (i, jj))
    (x,) = _matmul(o, w['w_o_gd'], j, [0], d, tm, 512, _epi_residual, [res], [F32], "gd_out")
    h = _rmsnorm([x], w['norm_ffn'][layer], act, tm)
    gates = _router(h, w['w_router'], j, tm)
    w_gu = w['w_gu_moe'][j]
    w_down = w['w_down_moe'][j]
    d_ffe = w_down.shape[1]
    if not decode:
        return x, _moe_routed(h, gates, w_gu, w_down), s_fin, proj
    h = h.astype(BF16)
    gate_x = (gates, (tm, N_EXPERTS), lambda jj, i: (i, 0))
    for e in range(N_EXPERTS):
        (hid,) = _matmul(h, w_gu, e, [0, d_ffe], d_ffe, tm, 256, _epi_swiglu, [], [BF16], "moe_gu")
        res = (x, (tm, 512), lambda jj, i: (i, jj))
        (x,) = _matmul(hid, w_down, e, [0], d, tm, 512, _epi_gated_residual(e), [res, gate_x], [F32],
                       "moe_down")
    return x, None, s_fin, proj


def kernel(x_prompt, x_sample, cache_k, cache_v, state_delta, state_conv, page_table, norm_mix, norm_ffn, norm_final, w_qkv_da, lambda_q1, lambda_k1, lambda_q2, lambda_k2, subln_da, w_o_da, w_in_gd, conv_w_gd, a_log_gd, dt_bias_gd, gnorm_gd, w_o_gd, w_gu_dense, w_down_dense, w_router, w_gu_moe, w_down_moe):
    w = dict(norm_mix=norm_mix, norm_ffn=norm_ffn, w_qkv_da=w_qkv_da, lambda_q1=lambda_q1,
             lambda_k1=lambda_k1, lambda_q2=lambda_q2, lambda_k2=lambda_k2, subln_da=subln_da,
             w_o_da=w_o_da, w_in_gd=w_in_gd, conv_w_gd=conv_w_gd, a_log_gd=a_log_gd,
             dt_bias_gd=dt_bias_gd, gnorm_gd=gnorm_gd, w_o_gd=w_o_gd, w_gu_dense=w_gu_dense,
             w_down_dense=w_down_dense, w_router=w_router, w_gu_moe=w_gu_moe, w_down_moe=w_down_moe)
    b, s, d = x_prompt.shape
    bd, t_dec, _ = x_sample.shape
    assert t_dec == 1
    n_pages = page_table.shape[1]
    past = n_pages * PAGE_SIZE
    tm_p = 512
    tm_s = bd * t_dec
    depth = norm_mix.shape[0]

    xp = x_prompt.reshape(b * s, d)
    xs = x_sample.reshape(bd * t_dec, d)
    cos_p, sin_p = _rope_tables(s, 0, s)
    cos_s, sin_s = _rope_tables(tm_s, past, t_dec)
    rope_p = (cos_p, sin_p, s // tm_p)
    rope_s = (cos_s, sin_s, 1)

    kp_rows, vp_rows, ks_rows, vs_rows = [], [], [], []
    sp_fin, cp_fin, ss_fin, cs_fin = [], [], [], []
    xp_terms = None
    for i in range(depth):
        j = i // 2
        if i % 2 == 0:
            def attend_prompt(q, k, v, lam_vecs, subln, lam_init):
                return _attn_prompt(q, k, v, lam_vecs, subln, b, s, lam_init)

            def attend_sample(q, k, v, lam_vecs, subln, lam_init, j=j):
                o = _attn_decode(q.reshape(bd, DA_SUB_HEADS, DA_HEAD_DIM), cache_k, cache_v, j, page_table,
                                 k.reshape(bd, DA_SUB_HEADS, DA_HEAD_DIM),
                                 v.reshape(bd, DA_HEADS, DA_V_DIM), lam_vecs, subln, lam_init)
                return o.reshape(bd, DA_HEADS * DA_V_DIM)

            xp, kp, vp = _attention_layer(xp, tm_p, i, j, w, rope_p, attend_prompt, False)
            xs, ks, vs = _attention_layer(xs, tm_s, i, j, w, rope_s, attend_sample, True)
            kp_rows.append(kp.reshape(b, s, DA_SUB_HEADS, DA_HEAD_DIM))
            vp_rows.append(vp.reshape(b, s, DA_HEADS, DA_V_DIM))
            ks_rows.append(ks.reshape(bd, t_dec, DA_SUB_HEADS, DA_HEAD_DIM))
            vs_rows.append(vs.reshape(bd, t_dec, DA_HEADS, DA_V_DIM))
        else:
            zero_buf = jnp.zeros((b * SUBLANES, GD_CONV_CH), F32)
            zero_state = jnp.zeros((b, GD_V_HEADS, GD_HEAD_DIM, GD_HEAD_DIM), F32)
            assert i == depth - 1
            xp, moe_p, sp, proj_p = _deltanet_layer(xp, tm_p, i, j, w, b, s, zero_buf, zero_state, False)
            xp_terms = [xp, moe_p]
            buf_s = jnp.pad(state_conv[j], ((0, 0), (SUBLANES - (CONV_WIDTH - 1), 0), (0, 0)))
            xs, _, ss, proj_s = _deltanet_layer(xs, tm_s, i, j, w, bd, t_dec,
                                                buf_s.reshape(bd * SUBLANES, GD_CONV_CH), state_delta[j], True)
            sp_fin.append(sp)
            ss_fin.append(ss)
            cp_fin.append(proj_p.reshape(b, s, -1)[:, s - (CONV_WIDTH - 1):, :GD_CONV_CH])
            cs_all = jnp.concatenate([state_conv[j], proj_s.reshape(bd, t_dec, -1)[:, :, :GD_CONV_CH]], axis=1)
            cs_fin.append(cs_all[:, t_dec:])
    y_prompt = _rmsnorm(xp_terms or [xp], norm_final, F32, tm_p).reshape(b, s, d)
    y_sample = _rmsnorm([xs], norm_final, F32, tm_s).reshape(bd, t_dec, d)
    return (y_prompt, y_sample,
            jnp.stack(kp_rows), jnp.stack(vp_rows), jnp.stack(ks_rows), jnp.stack(vs_rows),
            jnp.stack(sp_fin), jnp.stack(cp_fin), jnp.stack(ss_fin), jnp.stack(cs_fin))
```

```python
import functools
import math

import numpy as np
import jax
import jax.numpy as jnp
from jax import lax
from jax.experimental import pallas as pl
from jax.experimental.pallas import tpu as pltpu

F32 = jnp.float32
BF16 = jnp.bfloat16

PAGE_SIZE = 128
DA_HEAD_DIM = 64
DA_V_DIM = 128
DA_HEADS = 16
DA_SUB_HEADS = 32
DA_QK_WIDTH = 2048
ROPE_THETA = 500000.0
ROT_DIM = 16
LAMBDA_INIT_BASE = 0.8
LAMBDA_INIT_AMP = 0.6
LAMBDA_INIT_RATE = 0.3
GD_K_HEADS = 16
GD_V_HEADS = 32
GD_HEAD_DIM = 128
GD_KEY_WIDTH = 2048
GD_VAL_WIDTH = 4096
GD_CONV_CH = 8192
CONV_WIDTH = 4
CHUNK = 64
N_EXPERTS = 8
NORM_EPS = 1e-6
L2_EPS = 1e-6

LANES = 128
SUBLANES = 8
VMEM_LIMIT_BYTES = 56 * 1024 * 1024


def _params(n_grid_dims, vmem=VMEM_LIMIT_BYTES):
    return pltpu.CompilerParams(dimension_semantics=("arbitrary",) * n_grid_dims,
                                vmem_limit_bytes=vmem)


def _sigmoid(x):
    return 1.0 / (1.0 + jnp.exp(-x))


def _silu(x):
    return x * _sigmoid(x)


def _dot(a, b):
    return jnp.dot(a, b, preferred_element_type=F32)


def _dot_nt(a, b):
    return lax.dot_general(a, b, (((1,), (1,)), ((), ())), preferred_element_type=F32)


def _dot_tn(a, b):
    return lax.dot_general(a, b, (((0,), (0,)), ((), ())), preferred_element_type=F32)


def _dot_f32(a, b):
    return jnp.dot(a, b, preferred_element_type=F32, precision=lax.Precision.HIGHEST)


def _split_bf16(x):
    hi = x.astype(BF16)
    return hi, (x - hi.astype(F32)).astype(BF16)


def _rmsnorm_kernel(*refs):
    w_ref, o_ref = refs[-2:]
    x = refs[0][...]
    for extra in refs[1:-2]:
        x = x + extra[...]
    ms = jnp.mean(x * x, axis=-1, keepdims=True)
    o_ref[...] = (x * lax.rsqrt(ms + NORM_EPS) * w_ref[...]).astype(o_ref.dtype)


def _rmsnorm(xs, w, out_dtype, tm):
    m, d = xs[0].shape
    return pl.pallas_call(
        _rmsnorm_kernel,
        grid=(m // tm,),
        in_specs=[pl.BlockSpec((tm, d), lambda i: (i, 0)) for _ in xs]
                 + [pl.BlockSpec((1, d), lambda i: (0, 0))],
        out_specs=pl.BlockSpec((tm, d), lambda i: (i, 0)),
        out_shape=jax.ShapeDtypeStruct((m, d), out_dtype),
        compiler_params=_params(1),
        name="rmsnorm",
    )(*xs, w.reshape(1, d))


def _mm_kernel(*refs, n_w, n_x, n_o, epilogue, precise):
    a_ref = refs[0]
    w_refs = refs[1:1 + n_w]
    x_refs = refs[1 + n_w:1 + n_w + n_x]
    o_refs = refs[1 + n_w + n_x:1 + n_w + n_x + n_o]
    wbf_refs = refs[1 + n_w + n_x + n_o:]

    a = a_ref[...]
    if precise:
        accs = [_dot_f32(a, w_ref[...]) for w_ref in w_refs]
    else:
        @pl.when(pl.program_id(1) == 0)
        def _():
            for w_ref, wbf_ref in zip(w_refs, wbf_refs):
                wbf_ref[...] = w_ref[...].astype(BF16)

        accs = [_dot(a, wbf_ref[...]) for wbf_ref in wbf_refs]
    for o_ref, val in zip(o_refs, epilogue(accs, x_refs)):
        o_ref[...] = val.astype(o_ref.dtype)


def _matmul(a, w, widx, col_starts, n_cols, tm, tn, epilogue, extras, out_dtypes, name):
    m, k = a.shape
    precise = a.dtype == F32
    in_specs = [pl.BlockSpec((tm, k), lambda j, i: (i, 0))]
    for cs in col_starts:
        in_specs.append(pl.BlockSpec((None, k, tn), lambda j, i, cb=cs // tn: (widx, 0, cb + j)))
    for _, bs, im in extras:
        in_specs.append(pl.BlockSpec(bs, im))
    kernel = functools.partial(_mm_kernel, n_w=len(col_starts), n_x=len(extras),
                               n_o=len(out_dtypes), epilogue=epilogue, precise=precise)
    outs = pl.pallas_call(
        kernel,
        grid=(n_cols // tn, m // tm),
        in_specs=in_specs,
        out_specs=[pl.BlockSpec((tm, tn), lambda j, i: (i, j)) for _ in out_dtypes],
        out_shape=[jax.ShapeDtypeStruct((m, n_cols), dt) for dt in out_dtypes],
        scratch_shapes=[] if precise else [pltpu.VMEM((k, tn), BF16) for _ in col_starts],
        compiler_params=_params(2),
        name=name,
    )(a, *([w] * len(col_starts)), *[e[0] for e in extras])
    return outs


def _epi_plain(n_out):
    return lambda accs, xs: [accs[0]] * n_out


def _epi_residual(accs, xs):
    return [xs[0][...] + accs[0]]


def _epi_swiglu(accs, xs):
    return [_silu(accs[0]) * accs[1]]


def _epi_gated_residual(expert):
    def epi(accs, xs):
        gate = xs[1][...][:, expert:expert + 1]
        return [xs[0][...] + gate * accs[0]]
    return epi


def _apply_rope(acc, cos_ref, sin_ref):
    cos = cos_ref[...]
    sin = sin_ref[...]
    lane = lax.broadcasted_iota(jnp.int32, (1, LANES), 1) % DA_HEAD_DIM
    first_half = lane < ROT_DIM // 2
    cols = []
    for c in range(acc.shape[1] // LANES):
        x = acc[:, c * LANES:(c + 1) * LANES]
        partner = jnp.where(first_half, pltpu.roll(x, LANES - ROT_DIM // 2, 1),
                            pltpu.roll(x, ROT_DIM // 2, 1))
        cols.append(x * cos + partner * sin)
    return jnp.concatenate(cols, axis=1)


def _epi_rope(scales):
    def epi(accs, xs):
        r = _apply_rope(accs[0], xs[0], xs[1])
        return [r if s == 1.0 else r * s for s in scales]
    return epi


def _rope_table_kernel(inv_ref, sgn_ref, cos_ref, sin_ref, *, offset, period):
    rows = cos_ref.shape[0]
    t = lax.broadcasted_iota(jnp.int32, (rows, LANES), 0) % period
    pos = (t + offset).astype(F32)
    ang = pos * inv_ref[...]
    sgn = sgn_ref[...]
    cos_ref[...] = jnp.where(sgn != 0.0, jnp.cos(ang), 1.0)
    sin_ref[...] = sgn * jnp.sin(ang)


def _rope_tables(rows, offset, period):
    half = ROT_DIM // 2
    inv = ROPE_THETA ** (-2.0 * np.arange(half, dtype=np.float32) / ROT_DIM)
    lane = np.arange(LANES) % DA_HEAD_DIM
    inv_lane = np.where(lane < ROT_DIM, inv[lane % half], 0.0).astype(np.float32)
    sgn_lane = np.where(lane < half, -1.0, np.where(lane < ROT_DIM, 1.0, 0.0)).astype(np.float32)
    kernel = functools.partial(_rope_table_kernel, offset=offset, period=period)
    return pl.pallas_call(
        kernel,
        out_shape=[jax.ShapeDtypeStruct((rows, LANES), F32)] * 2,
        name="rope_tables",
    )(jnp.asarray(inv_lane).reshape(1, LANES), jnp.asarray(sgn_lane).reshape(1, LANES))


def _diff_lambda(lam_ref, lam_init):
    lv = lam_ref[...]
    a = jnp.sum(lv[0:1] * lv[1:2], axis=-1, keepdims=True)
    b = jnp.sum(lv[2:3] * lv[3:4], axis=-1, keepdims=True)
    return jnp.exp(a) - jnp.exp(b) + lam_init


def _sub_layer_norm(o, subln_ref, lam_init):
    ms = jnp.mean(o * o, axis=-1, keepdims=True)
    return o * lax.rsqrt(ms + NORM_EPS) * subln_ref[...] * (1.0 - lam_init)


def _attn_prompt_kernel(lam_ref, subln_ref, q_ref, k_ref, v_ref, o_ref, m_sc, l_sc, acc_sc,
                        *, tq, tk, lam_init, heads):
    qi = pl.program_id(2)
    ki = pl.program_id(3)

    @pl.when(ki == 0)
    def _():
        m_sc[...] = jnp.full(m_sc.shape, -jnp.inf, F32)
        l_sc[...] = jnp.zeros(l_sc.shape, F32)
        acc_sc[...] = jnp.zeros(acc_sc.shape, F32)

    def accumulate(masked):
        lane = lax.broadcasted_iota(jnp.int32, (1, LANES), 1)
        if masked:
            row = lax.broadcasted_iota(jnp.int32, (tq, tk), 0)
            col = lax.broadcasted_iota(jnp.int32, (tq, tk), 1)
            causal = col <= row
        for hh in range(heads):
            cols = slice(hh * LANES, (hh + 1) * LANES)
            q = q_ref[:, cols]
            k = k_ref[:, cols]
            v = v_ref[:, cols]
            zero = jnp.zeros_like(q)
            for sub in range(2):
                idx = 2 * hh + sub
                in_sub = (lane < DA_HEAD_DIM) if sub == 0 else (lane >= DA_HEAD_DIM)
                s = _dot_nt(jnp.where(in_sub, q, zero), k)
                if masked:
                    s = jnp.where(causal, s, -jnp.inf)
                m_prev = m_sc[idx]
                m_new = jnp.maximum(m_prev, jnp.max(s, axis=-1, keepdims=True))
                alpha = jnp.exp(m_prev - m_new)
                p = jnp.exp(s - m_new[:, :1])
                l_sc[idx] = alpha * l_sc[idx] + jnp.sum(p, axis=-1, keepdims=True)
                acc_sc[idx] = alpha * acc_sc[idx] + _dot(p.astype(BF16), v)
                m_sc[idx] = m_new

    @pl.when(ki < qi)
    def _():
        accumulate(False)

    @pl.when(ki == qi)
    def _():
        accumulate(True)
        lam = _diff_lambda(lam_ref, lam_init)
        for hh in range(heads):
            o = acc_sc[2 * hh] / l_sc[2 * hh] - lam * (acc_sc[2 * hh + 1] / l_sc[2 * hh + 1])
            o_ref[:, hh * LANES:(hh + 1) * LANES] = _sub_layer_norm(o, subln_ref, lam_init).astype(o_ref.dtype)


def _attn_prompt(q, k, v, lam_vecs, subln, batch, seq, lam_init, tq=512, heads=4):
    nq = seq // tq
    width = heads * LANES
    kernel = functools.partial(_attn_prompt_kernel, tq=tq, tk=tq, lam_init=lam_init, heads=heads)
    kv_spec = pl.BlockSpec((tq, width), lambda b, h, qi, ki: (b * nq + jnp.minimum(ki, qi), h))
    return pl.pallas_call(
        kernel,
        grid=(batch, DA_HEADS // heads, nq, nq),
        in_specs=[pl.BlockSpec((SUBLANES, LANES), lambda b, h, qi, ki: (0, 0)),
                  pl.BlockSpec((1, LANES), lambda b, h, qi, ki: (0, 0)),
                  pl.BlockSpec((tq, width), lambda b, h, qi, ki: (b * nq + qi, h)),
                  kv_spec, kv_spec],
        out_specs=pl.BlockSpec((tq, width), lambda b, h, qi, ki: (b * nq + qi, h)),
        out_shape=jax.ShapeDtypeStruct((batch * seq, DA_HEADS * DA_V_DIM), BF16),
        scratch_shapes=[pltpu.VMEM((2 * heads, tq, LANES), F32) for _ in range(3)],
        compiler_params=_params(4),
        name="attn_prompt",
    )(lam_vecs, subln, q, k, v)


def _attn_decode_kernel(pt_ref, lam_ref, subln_ref, spread_ref, qrep_ref, q_ref, kn_ref, vn_ref, *rest,
                        lam_init, n_blk):
    del pt_ref
    k_refs = rest[:n_blk]
    v_refs = rest[n_blk:2 * n_blk]
    o_ref, m_sc, l_sc, acc_sc = rest[2 * n_blk:]
    p = pl.program_id(1)

    @pl.when(p == 0)
    def _():
        m_sc[...] = jnp.full(m_sc.shape, -jnp.inf, F32)
        l_sc[...] = jnp.zeros(l_sc.shape, F32)
        acc_sc[...] = jnp.zeros(acc_sc.shape, F32)

    def update(s, weighted_values):
        m_prev = m_sc[...]
        m_new = jnp.maximum(m_prev, jnp.max(s, axis=-1, keepdims=True))
        alpha = jnp.exp(m_prev - m_new)
        pe = jnp.exp(s - m_new[:, :1])
        l_sc[...] = alpha * l_sc[...] + jnp.sum(pe, axis=-1, keepdims=True)
        acc_sc[...] = alpha * acc_sc[...] + weighted_values(pe)
        m_sc[...] = m_new

    width = PAGE_SIZE * DA_HEADS
    row_head = (lax.broadcasted_iota(jnp.int32, (2 * DA_SUB_HEADS, width), 0) % DA_SUB_HEADS) // 2
    col_head = lax.broadcasted_iota(jnp.int32, (2 * DA_SUB_HEADS, width), 1) % DA_HEADS
    own_head = row_head == col_head
    q_rep = qrep_ref[...]
    for k_ref, v_ref in zip(k_refs, v_refs):
        s = jnp.sum(k_ref[...] * q_rep, axis=1)

        def page_values(pe, v_ref=v_ref):
            p_hi, p_lo = _split_bf16(pe)
            spread = _dot(jnp.concatenate([p_hi, p_lo], axis=0), spread_ref[...])
            pm = jnp.where(own_head, spread, 0.0).astype(BF16)
            v_hi, v_lo = _split_bf16(v_ref[...].reshape(width, DA_V_DIM))
            both = _dot(pm, v_hi)
            return both[:DA_SUB_HEADS] + both[DA_SUB_HEADS:] + _dot(pm[:DA_SUB_HEADS], v_lo)

        update(s, page_values)

    @pl.when(p == pl.num_programs(1) - 1)
    def _():
        s_new = jnp.sum(q_ref[...] * kn_ref[...], axis=-1, keepdims=True)
        update(s_new, lambda pe: pe * vn_ref[...])
        lam = _diff_lambda(lam_ref, lam_init)
        acc_sc[...] = acc_sc[...] / l_sc[...]
        o = (acc_sc[pl.ds(0, DA_HEADS, stride=2), :]
             - lam * acc_sc[pl.ds(1, DA_HEADS, stride=2), :])
        o_ref[...] = _sub_layer_norm(o, subln_ref, lam_init).astype(o_ref.dtype)


def _attn_decode(q, cache_k, cache_v, layer, page_table, k_new, v_new, lam_vecs, subln, lam_init,
                 pages_per_step=4):
    bd, n_pages = page_table.shape
    n_blk = pages_per_step
    assert n_pages % n_blk == 0
    keys_t = jnp.transpose(cache_k, (0, 1, 3, 4, 2))
    q_rep = jnp.broadcast_to(q[..., None], q.shape + (PAGE_SIZE,))
    v_new2 = jnp.repeat(v_new, 2, axis=1)
    width = PAGE_SIZE * DA_HEADS
    spread = jnp.asarray(np.arange(width)[None, :] // DA_HEADS == np.arange(PAGE_SIZE)[:, None], BF16)
    kernel = functools.partial(_attn_decode_kernel, lam_init=lam_init, n_blk=n_blk)
    per_seq = lambda *dims: pl.BlockSpec((None,) + dims, lambda b, p, pt: (b,) + (0,) * len(dims))
    page = lambda dims, i: pl.BlockSpec(
        (None, None) + dims, lambda b, p, pt: (layer, pt[b, p * n_blk + i], 0, 0, 0))
    grid_spec = pltpu.PrefetchScalarGridSpec(
        num_scalar_prefetch=1,
        grid=(bd, n_pages // n_blk),
        in_specs=[
            pl.BlockSpec((SUBLANES, LANES), lambda b, p, pt: (0, 0)),
            pl.BlockSpec((1, LANES), lambda b, p, pt: (0, 0)),
            pl.BlockSpec((PAGE_SIZE, width), lambda b, p, pt: (0, 0)),
            per_seq(DA_SUB_HEADS, DA_HEAD_DIM, PAGE_SIZE),
            per_seq(DA_SUB_HEADS, DA_HEAD_DIM),
            per_seq(DA_SUB_HEADS, DA_HEAD_DIM),
            per_seq(DA_SUB_HEADS, DA_V_DIM),
        ] + [page((DA_SUB_HEADS, DA_HEAD_DIM, PAGE_SIZE), i) for i in range(n_blk)]
          + [page((PAGE_SIZE, DA_HEADS, DA_V_DIM), i) for i in range(n_blk)],
        out_specs=per_seq(DA_HEADS, DA_V_DIM),
        scratch_shapes=[pltpu.VMEM((DA_SUB_HEADS, LANES), F32), pltpu.VMEM((DA_SUB_HEADS, LANES), F32),
                        pltpu.VMEM((DA_SUB_HEADS, DA_V_DIM), F32)],
    )
    return pl.pallas_call(
        kernel,
        grid_spec=grid_spec,
        out_shape=jax.ShapeDtypeStruct((bd, DA_HEADS, DA_V_DIM), F32),
        compiler_params=_params(2),
        name="attn_decode",
    )(page_table, lam_vecs, subln, spread, q_rep, q, k_new, v_new2,
      *([keys_t] * n_blk), *([cache_v] * n_blk))


def _gdn_gates_kernel(ba_ref, alog_ref, dtb_ref, o_ref):
    x = ba_ref[...]
    lane = lax.broadcasted_iota(jnp.int32, x.shape, 1)
    beta = _sigmoid(x)
    y = x + dtb_ref[...]
    softplus = jnp.maximum(y, 0.0) + jnp.log1p(jnp.exp(-jnp.abs(y)))
    g = -jnp.exp(alog_ref[...]) * softplus
    o_ref[...] = jnp.where(lane < GD_V_HEADS, beta, jnp.where(lane < 2 * GD_V_HEADS, g, 0.0))


def _gdn_gates(ba, a_log, dt_bias, tm):
    m = ba.shape[0]
    pad = lambda v: jnp.pad(v.reshape(1, GD_V_HEADS), ((0, 0), (GD_V_HEADS, LANES - 2 * GD_V_HEADS)))
    return pl.pallas_call(
        _gdn_gates_kernel,
        grid=(m // tm,),
        in_specs=[pl.BlockSpec((tm, LANES), lambda i: (i, 0)),
                  pl.BlockSpec((1, LANES), lambda i: (0, 0)),
                  pl.BlockSpec((1, LANES), lambda i: (0, 0))],
        out_specs=pl.BlockSpec((tm, LANES), lambda i: (i, 0)),
        out_shape=jax.ShapeDtypeStruct((m, LANES), F32),
        compiler_params=_params(1),
        name="gdn_gates",
    )(ba, pad(a_log), pad(dt_bias))


def _gdn_conv_kernel(x_ref, buf_ref, w_ref, o_ref, xin_sc, *, heads):
    t = x_ref.shape[0]
    xin_sc[0:SUBLANES, :] = buf_ref[...]
    xin_sc[SUBLANES:SUBLANES + t, :] = x_ref[...]
    for hh in range(heads):
        c = pl.program_id(1) * heads + hh
        cols = slice(hh * LANES, (hh + 1) * LANES)
        w = w_ref[:, cols]
        y = x_ref[:, cols] * w[CONV_WIDTH - 1:CONV_WIDTH]
        for s in range(1, CONV_WIDTH):
            shifted = xin_sc[pl.ds(SUBLANES - s, t), cols]
            y = y + shifted * w[CONV_WIDTH - 1 - s:CONV_WIDTH - s]
        y = _silu(y)
        inv = lax.rsqrt(jnp.sum(y * y, axis=-1, keepdims=True) + L2_EPS)
        scale = jnp.where(c < GD_K_HEADS, GD_HEAD_DIM ** -0.5, 1.0)
        o_ref[:, cols] = jnp.where(c < 2 * GD_K_HEADS, y * (inv * scale), y)


def _gdn_conv(x, buf, conv_w, widx, batch, t, heads):
    width = heads * LANES
    return pl.pallas_call(
        functools.partial(_gdn_conv_kernel, heads=heads),
        grid=(batch, GD_CONV_CH // width),
        in_specs=[pl.BlockSpec((t, width), lambda b, c: (b, c)),
                  pl.BlockSpec((SUBLANES, width), lambda b, c: (b, c)),
                  pl.BlockSpec((None, CONV_WIDTH, width), lambda b, c: (widx, 0, c))],
        out_specs=pl.BlockSpec((t, width), lambda b, c: (b, c)),
        out_shape=jax.ShapeDtypeStruct((batch * t, GD_CONV_CH), F32),
        scratch_shapes=[pltpu.VMEM((SUBLANES + t, width), F32)],
        compiler_params=_params(2),
        name="gdn_conv",
    )(x, buf, conv_w)


def _split_products(lhs_list, rhs_list):
    ls = [_split_bf16(a) for a in lhs_list]
    rs = [_split_bf16(b) for b in rhs_list]
    first = [_dot(jnp.concatenate([a_hi, a_lo], axis=0), b_hi) for (a_hi, a_lo), (b_hi, _) in zip(ls, rs)]
    second = [_dot(a_hi, b_lo) for (a_hi, _), (_, b_lo) in zip(ls, rs)]
    outs = []
    for f, s, a in zip(first, second, lhs_list):
        n = a.shape[0]
        outs.append(f[:n] + f[n:] + s)
    return outs


def _unit_lower_inverses(ms, eye):
    n = eye.shape[0]
    invs = [eye - m for m in ms]
    powers = _split_products(ms, ms)
    n_factors = int(math.log2(CHUNK)) - 1
    for f in range(n_factors):
        last = f + 1 == n_factors
        lhs = invs if last else [jnp.concatenate([i, p], axis=0) for i, p in zip(invs, powers)]
        prods = _split_products(lhs, powers)
        invs = [i + pr[:n] for i, pr in zip(invs, prods)]
        if not last:
            powers = [pr[n:] for pr in prods]
    return invs


def _gdn_chunk_kernel(q_ref, k_ref, v_ref, z_ref, bg_ref, gt_ref, gn_ref, s0_ref, o_ref, sfin_ref,
                      s_sc, *, heads):
    grp = pl.program_id(1)
    c = pl.program_id(2)
    pairs = heads // 2
    two = 2 * CHUNK

    @pl.when(c == 0)
    def _():
        s_sc[...] = s0_ref[...].reshape(s_sc.shape)

    bg = bg_ref[...]
    gn = gn_ref[...]
    lane = lax.broadcasted_iota(jnp.int32, (1, LANES), 1)
    ri = lax.broadcasted_iota(jnp.int32, (two, two), 0)
    ci = lax.broadcasted_iota(jnp.int32, (two, two), 1)
    same = (ri // CHUNK) == (ci // CHUNK)
    tril = same & (ri >= ci)
    triu = same & (ri <= ci)
    strict = same & (ri > ci)
    eye = (ri == ci).astype(F32)
    top_rows = lax.broadcasted_iota(jnp.int32, (two, 1), 0) < CHUNK
    top_state = lax.broadcasted_iota(jnp.int32, (2 * GD_HEAD_DIM, 1), 0) < GD_HEAD_DIM

    def stack2(x):
        return jnp.concatenate([x, x], axis=0)

    def blocks(x):
        zero = jnp.zeros_like(x)
        return jnp.concatenate([jnp.where(top_rows, x, zero), jnp.where(top_rows, zero, x)], axis=1)

    def column(lane_idx):
        return jnp.sum(jnp.where(lane == lane_idx, bg, 0.0), axis=-1, keepdims=True)

    qs = [stack2(q_ref[:, kh * LANES:(kh + 1) * LANES]) for kh in range(pairs)]
    ks = [stack2(k_ref[:, kh * LANES:(kh + 1) * LANES]) for kh in range(pairs)]
    vs = [jnp.concatenate([v_ref[:, (2 * kh) * LANES:(2 * kh + 1) * LANES],
                           v_ref[:, (2 * kh + 1) * LANES:(2 * kh + 2) * LANES]], axis=0)
          for kh in range(pairs)]
    kq = [_dot_nt(jnp.concatenate([k, q], axis=0).astype(BF16), k.astype(BF16)) for k, q in zip(ks, qs)]

    betas, gcums, decays, egs = [], [], [], []
    for kh in range(pairs):
        h = grp * heads + 2 * kh
        beta = jnp.concatenate([column(h), column(h + 1)], axis=0)
        g_col = jnp.concatenate([column(h + GD_V_HEADS), column(h + 1 + GD_V_HEADS)], axis=0)
        g_row = gt_ref[kh:kh + 1, :]
        gcum_col = jnp.sum(jnp.where(tril, g_row, 0.0), axis=-1, keepdims=True)
        gcum_row = jnp.sum(jnp.where(triu, g_col, 0.0), axis=0, keepdims=True)
        betas.append(beta)
        gcums.append(gcum_col)
        decays.append(jnp.exp(jnp.where(tril, gcum_col - gcum_row, -jnp.inf)))
        egs.append(jnp.exp(gcum_col))

    tinvs = _unit_lower_inverses(
        [jnp.where(strict, b * x[:two] * d, 0.0) for b, x, d in zip(betas, kq, decays)], eye)
    rhs = [jnp.concatenate([v * b, k * (b * e)], axis=1).astype(BF16)
           for v, k, b, e in zip(vs, ks, betas, egs)]
    wus = [_dot(t.astype(BF16), r) for t, r in zip(tinvs, rhs)]
    s_olds = [s_sc[kh] for kh in range(pairs)]
    s_bfs = [s.astype(BF16) for s in s_olds]
    v_news = [wu[:, :LANES] - _dot(blocks(wu[:, LANES:]).astype(BF16), s)
              for wu, s in zip(wus, s_bfs)]
    v_bfs = [v.astype(BF16) for v in v_news]
    outs = [_dot(jnp.concatenate([blocks(q * e), x[two:] * d], axis=1).astype(BF16),
                 jnp.concatenate([s, v], axis=0))
            for q, e, x, d, s, v in zip(qs, egs, kq, decays, s_bfs, v_bfs)]
    g_lasts = [jnp.where(top_rows, g[CHUNK - 1:CHUNK, :], g[two - 1:two, :]) for g in gcums]
    updates = [_dot_tn(blocks(k * jnp.exp(gl - g)).astype(BF16), v)
               for k, gl, g, v in zip(ks, g_lasts, gcums, v_bfs)]
    for kh in range(pairs):
        g = gcums[kh]
        carry = jnp.exp(jnp.where(top_state, g[CHUNK - 1:CHUNK, :], g[two - 1:two, :]))
        s_sc[kh] = s_olds[kh] * carry + updates[kh]
        o = outs[kh]
        z = jnp.concatenate([z_ref[:, (2 * kh) * LANES:(2 * kh + 1) * LANES],
                             z_ref[:, (2 * kh + 1) * LANES:(2 * kh + 2) * LANES]], axis=0)
        ms = jnp.mean(o * o, axis=-1, keepdims=True)
        on = (o * lax.rsqrt(ms + NORM_EPS) * gn * _silu(z)).astype(o_ref.dtype)
        o_ref[:, (2 * kh) * LANES:(2 * kh + 1) * LANES] = on[:CHUNK]
        o_ref[:, (2 * kh + 1) * LANES:(2 * kh + 2) * LANES] = on[CHUNK:]

    @pl.when(c == pl.num_programs(2) - 1)
    def _():
        sfin_ref[...] = s_sc[...].reshape(sfin_ref.shape)


def _gdn_chunks(qkv, proj, bg, g_t, gnorm, s0, batch, t, heads=16):
    nc = t // CHUNK
    n_grp = GD_V_HEADS // heads
    pairs = heads // 2
    kw = pairs * LANES
    vw = heads * LANES
    row = lambda b, g, c: b * nc + c
    kernel = functools.partial(_gdn_chunk_kernel, heads=heads)
    return pl.pallas_call(
        kernel,
        grid=(batch, n_grp, nc),
        in_specs=[
            pl.BlockSpec((CHUNK, kw), lambda b, g, c: (row(b, g, c), g)),
            pl.BlockSpec((CHUNK, kw), lambda b, g, c: (row(b, g, c), GD_KEY_WIDTH // kw + g)),
            pl.BlockSpec((CHUNK, vw), lambda b, g, c: (row(b, g, c), 2 * GD_KEY_WIDTH // vw + g)),
            pl.BlockSpec((CHUNK, vw), lambda b, g, c: (row(b, g, c), GD_CONV_CH // vw + g)),
            pl.BlockSpec((CHUNK, LANES), lambda b, g, c: (row(b, g, c), 0)),
            pl.BlockSpec((None, None, pairs, 2 * CHUNK), lambda b, g, c: (b, c, g, 0)),
            pl.BlockSpec((1, LANES), lambda b, g, c: (0, 0)),
            pl.BlockSpec((None, heads, GD_HEAD_DIM, GD_HEAD_DIM), lambda b, g, c: (b, g, 0, 0)),
        ],
        out_specs=[
            pl.BlockSpec((CHUNK, vw), lambda b, g, c: (row(b, g, c), g)),
            pl.BlockSpec((None, heads, GD_HEAD_DIM, GD_HEAD_DIM), lambda b, g, c: (b, g, 0, 0)),
        ],
        out_shape=[jax.ShapeDtypeStruct((batch * t, GD_VAL_WIDTH), BF16),
                   jax.ShapeDtypeStruct((batch, GD_V_HEADS, GD_HEAD_DIM, GD_HEAD_DIM), F32)],
        scratch_shapes=[pltpu.VMEM((pairs, 2 * GD_HEAD_DIM, GD_HEAD_DIM), F32)],
        compiler_params=_params(3),
        name="gdn_chunks",
    )(qkv, qkv, qkv, proj, bg, g_t, gnorm, s0)


def _gdn_step_kernel(qkv_ref, z_ref, bg_ref, gn_ref, s0_ref, o_ref, sfin_ref):
    ri = lax.broadcasted_iota(jnp.int32, (GD_HEAD_DIM, GD_HEAD_DIM), 0)
    ci = lax.broadcasted_iota(jnp.int32, (GD_HEAD_DIM, GD_HEAD_DIM), 1)
    diag = ri == ci
    bg = bg_ref[...]
    gn = gn_ref[...]

    def column(row_vec):
        return jnp.sum(jnp.where(diag, row_vec, 0.0), axis=-1, keepdims=True)

    for kh in range(GD_K_HEADS):
        q_col = column(qkv_ref[:, kh * LANES:(kh + 1) * LANES])
        k_col = column(qkv_ref[:, GD_KEY_WIDTH + kh * LANES:GD_KEY_WIDTH + (kh + 1) * LANES])
        for hh in (2 * kh, 2 * kh + 1):
            v = qkv_ref[:, 2 * GD_KEY_WIDTH + hh * LANES:2 * GD_KEY_WIDTH + (hh + 1) * LANES]
            beta = bg[:, hh:hh + 1]
            eg = jnp.exp(bg[:, GD_V_HEADS + hh:GD_V_HEADS + hh + 1])
            s_old = s0_ref[hh]
            sk = jnp.sum(s_old * k_col, axis=0, keepdims=True)
            v_new = beta * (v - eg * sk)
            s_new = eg * s_old + k_col * v_new
            sfin_ref[hh] = s_new
            o = jnp.sum(s_new * q_col, axis=0, keepdims=True)
            z = z_ref[:, hh * LANES:(hh + 1) * LANES]
            ms = jnp.mean(o * o, axis=-1, keepdims=True)
            o_ref[:, hh * LANES:(hh + 1) * LANES] = o * lax.rsqrt(ms + NORM_EPS) * gn * _silu(z)


def _gdn_step(qkv, z, bg, gnorm, s0):
    batch = qkv.shape[0]
    row = lambda width: pl.BlockSpec((None, 1, width), lambda b: (b, 0, 0))
    state = pl.BlockSpec((None, GD_V_HEADS, GD_HEAD_DIM, GD_HEAD_DIM), lambda b: (b, 0, 0, 0))
    return pl.pallas_call(
        _gdn_step_kernel,
        grid=(batch,),
        in_specs=[row(GD_CONV_CH), row(GD_VAL_WIDTH), row(LANES),
                  pl.BlockSpec((1, LANES), lambda b: (0, 0)), state],
        out_specs=[row(GD_VAL_WIDTH), state],
        out_shape=[jax.ShapeDtypeStruct((batch, 1, GD_VAL_WIDTH), F32),
                   jax.ShapeDtypeStruct(s0.shape, F32)],
        compiler_params=_params(1),
        name="gdn_step",
    )(qkv, z, bg, gnorm, s0)


def _router_kernel(h_ref, w_ref, o_ref):
    if h_ref.dtype == F32:
        logits = _dot_f32(h_ref[...], w_ref[...])
    else:
        logits = _dot(h_ref[...], w_ref[...].astype(BF16))
    e = jnp.exp(logits - jnp.max(logits, axis=-1, keepdims=True))
    probs = e / jnp.sum(e, axis=-1, keepdims=True)
    lane = lax.broadcasted_iota(jnp.int32, probs.shape, 1)
    v1 = jnp.max(probs, axis=-1, keepdims=True)
    i1 = jnp.min(jnp.where(probs == v1, lane, N_EXPERTS), axis=-1, keepdims=True)
    rest = jnp.where(lane == i1, -1.0, probs)
    v2 = jnp.max(rest, axis=-1, keepdims=True)
    i2 = jnp.min(jnp.where(rest == v2, lane, N_EXPERTS), axis=-1, keepdims=True)
    total = v1 + v2
    o_ref[...] = jnp.where(lane == i1, v1 / total, 0.0) + jnp.where(lane == i2, v2 / total, 0.0)


def _router(h, w_router, widx, tm):
    m, d = h.shape
    return pl.pallas_call(
        _router_kernel,
        grid=(m // tm,),
        in_specs=[pl.BlockSpec((tm, d), lambda i: (i, 0)),
                  pl.BlockSpec((None, d, N_EXPERTS), lambda i: (widx, 0, 0))],
        out_specs=pl.BlockSpec((tm, N_EXPERTS), lambda i: (i, 0)),
        out_shape=jax.ShapeDtypeStruct((m, N_EXPERTS), F32),
        compiler_params=_params(1),
        name="moe_router",
    )(h, w_router)


MOE_BLOCK = 1024
MOE_SUB = 128
MOE_FF_CHUNK = 256


def _moe_kernel(cnt_ref, h_ref, gc_ref, gr_ref, wg_ref, wu_ref, wd_ref, o_ref,
                xg_sc, y_sc, rc_sc, rr_sc, gate_sc):
    i = pl.program_id(0)
    e = pl.program_id(1)
    f = pl.program_id(2)
    tb = h_ref.shape[0]
    n_sub = tb // MOE_SUB
    cnt = cnt_ref[i, e]
    rows = lambda s: slice(s * MOE_SUB, (s + 1) * MOE_SUB)

    @pl.when((e == 0) & (f == 0))
    def _():
        o_ref[...] = jnp.zeros(o_ref.shape, F32)
        ri = lax.broadcasted_iota(jnp.int32, (MOE_SUB, MOE_SUB), 0)
        ci = lax.broadcasted_iota(jnp.int32, (MOE_SUB, MOE_SUB), 1)
        before_c = jnp.where(ci < ri, 1.0, 0.0).astype(BF16)
        before_r = jnp.where(ri < ci, 1.0, 0.0).astype(BF16)
        off_c = jnp.zeros((1, N_EXPERTS), F32)
        off_r = jnp.zeros((N_EXPERTS, 1), F32)
        for s in range(n_sub):
            sel_c = jnp.where(gc_ref[rows(s), :] > 0.0, 1.0, 0.0)
            rc_sc[rows(s), :] = _dot(before_c, sel_c.astype(BF16)) + off_c
            off_c = off_c + jnp.sum(sel_c, axis=0, keepdims=True)
            sel_r = jnp.where(gr_ref[:, rows(s)] > 0.0, 1.0, 0.0)
            rr_sc[:, rows(s)] = _dot(sel_r.astype(BF16), before_r) + off_r
            off_r = off_r + jnp.sum(sel_r, axis=1, keepdims=True)

    n_act = (cnt + MOE_SUB - 1) // MOE_SUB
    tiers = [t for t in (1, 2, 3, 4) if t < n_sub] + [n_sub]

    def per_tier(body):
        below = 0
        for tier in tiers:
            @pl.when((n_act > below) & (n_act <= tier))
            def _():
                body(tier * MOE_SUB)
            below = tier

    @pl.when(f == 0)
    def _():
        rank_r = rr_sc[pl.ds(e, 1), :]
        gate_r = gr_ref[pl.ds(e, 1), :]

        def gather(n_rows):
            slot = lax.broadcasted_iota(jnp.int32, (n_rows, 1), 0).astype(F32)
            pick = (rank_r == slot) & (gate_r > 0.0)
            xg_sc[0:n_rows, :] = _dot(jnp.where(pick, 1.0, 0.0).astype(BF16),
                                      h_ref[...]).astype(BF16)
            gate_sc[0:n_rows, :] = jnp.sum(jnp.where(pick, gate_r, 0.0), axis=1, keepdims=True)
            y_sc[0:n_rows, :] = jnp.zeros((n_rows, y_sc.shape[1]), F32)

        per_tier(gather)

    def expert(n_rows):
        xs = xg_sc[0:n_rows, :]
        hid = _silu(_dot(xs, wg_ref[...])) * _dot(xs, wu_ref[...])
        y_sc[0:n_rows, :] += _dot(hid.astype(BF16), wd_ref[...])

    per_tier(expert)

    @pl.when(f == pl.num_programs(2) - 1)
    def _():
        lane = lax.broadcasted_iota(jnp.int32, (1, N_EXPERTS), 1)
        rank_c = jnp.sum(jnp.where(lane == e, rc_sc[...], 0.0), axis=1, keepdims=True)
        gate_c = jnp.sum(jnp.where(lane == e, gc_ref[...], 0.0), axis=1, keepdims=True)
        for s in range(n_sub):
            @pl.when(s * MOE_SUB < cnt)
            def _():
                y_hi, y_lo = _split_bf16(y_sc[rows(s), :] * gate_sc[rows(s), :])
                slot = (lax.broadcasted_iota(jnp.int32, (1, MOE_SUB), 1) + s * MOE_SUB).astype(F32)
                place = jnp.where((rank_c == slot) & (gate_c > 0.0), 1.0, 0.0).astype(BF16)
                o_ref[...] += _dot(jnp.concatenate([place, place], axis=1),
                                   jnp.concatenate([y_hi, y_lo], axis=0))


def _moe_routed(h, gates, w_gu, w_down):
    m, d = h.shape
    n_exp, f_dim, _ = w_down.shape
    nb = m // MOE_BLOCK
    nf = f_dim // MOE_FF_CHUNK
    counts = jnp.sum((gates > 0.0).reshape(nb, MOE_BLOCK, n_exp), axis=1).astype(jnp.int32)
    w_gu_bf = w_gu.astype(BF16).reshape(n_exp, d, 2 * nf, MOE_FF_CHUNK).transpose(0, 2, 1, 3)
    w_down_bf = w_down.astype(BF16)
    grid_spec = pltpu.PrefetchScalarGridSpec(
        num_scalar_prefetch=1,
        grid=(nb, n_exp, nf),
        in_specs=[
            pl.BlockSpec((MOE_BLOCK, d), lambda i, e, f, c: (i, 0)),
            pl.BlockSpec((MOE_BLOCK, n_exp), lambda i, e, f, c: (i, 0)),
            pl.BlockSpec((n_exp, MOE_BLOCK), lambda i, e, f, c: (0, i)),
            pl.BlockSpec((None, None, d, MOE_FF_CHUNK), lambda i, e, f, c: (e, f, 0, 0)),
            pl.BlockSpec((None, None, d, MOE_FF_CHUNK), lambda i, e, f, c: (e, nf + f, 0, 0)),
            pl.BlockSpec((None, MOE_FF_CHUNK, d), lambda i, e, f, c: (e, f, 0)),
        ],
        out_specs=pl.BlockSpec((MOE_BLOCK, d), lambda i, e, f, c: (i, 0)),
        scratch_shapes=[pltpu.VMEM((MOE_BLOCK, d), BF16), pltpu.VMEM((MOE_BLOCK, d), F32),
                        pltpu.VMEM((MOE_BLOCK, n_exp), F32), pltpu.VMEM((n_exp, MOE_BLOCK), F32),
                        pltpu.VMEM((MOE_BLOCK, 1), F32)],
    )
    return pl.pallas_call(
        _moe_kernel,
        grid_spec=grid_spec,
        out_shape=jax.ShapeDtypeStruct((m, d), F32),
        compiler_params=_params(3),
        name="moe_routed",
    )(counts, h, gates, gates.T, w_gu_bf, w_gu_bf, w_down_bf)


def _lambda_vectors(lq1, lk1, lq2, lk2):
    rows = jnp.stack([lq1, lk1, lq2, lk2]).astype(F32)
    return jnp.pad(rows, ((0, SUBLANES - 4), (0, LANES - DA_HEAD_DIM)))


def _attention_layer(x, tm, layer, j, w, rope, attend, decode):
    d = x.shape[1]
    act = F32 if decode else BF16
    lam_init = LAMBDA_INIT_BASE - LAMBDA_INIT_AMP * math.exp(-LAMBDA_INIT_RATE * layer)
    lam_vecs = _lambda_vectors(w['lambda_q1'][j], w['lambda_k1'][j], w['lambda_q2'][j], w['lambda_k2'][j])
    subln = w['subln_da'][j].reshape(1, DA_V_DIM)
    cos, sin, n_tab = rope
    tabs = [(arr, (tm, LANES), lambda jj, i: (i % n_tab, 0)) for arr in (cos, sin)]
    h = _rmsnorm([x], w['norm_mix'][layer], act, tm)
    tn = 512
    kv_dtypes = [F32] if decode else [F32, BF16]
    (q,) = _matmul(h, w['w_qkv_da'], j, [0], DA_QK_WIDTH, tm, tn,
                   _epi_rope([DA_HEAD_DIM ** -0.5]), tabs, [act], "da_q")
    k = _matmul(h, w['w_qkv_da'], j, [DA_QK_WIDTH], DA_QK_WIDTH, tm, tn,
                _epi_rope([1.0] * len(kv_dtypes)), tabs, kv_dtypes, "da_k")
    v = _matmul(h, w['w_qkv_da'], j, [2 * DA_QK_WIDTH], DA_HEADS * DA_V_DIM, tm, tn,
                _epi_plain(len(kv_dtypes)), [], kv_dtypes, "da_v")
    o = attend(q, k[-1], v[-1], lam_vecs, subln, lam_init)
    res = (x, (tm, tn), lambda jj, i: (i, jj))
    (x,) = _matmul(o, w['w_o_da'], j, [0], d, tm, tn, _epi_residual, [res], [F32], "da_out")
    h = _rmsnorm([x], w['norm_ffn'][layer], act, tm)
    d_ff = w['w_down_dense'].shape[1]
    (hid,) = _matmul(h, w['w_gu_dense'], j, [0, d_ff], d_ff, tm, tn, _epi_swiglu, [], [act], "ffn_gu")
    tm_down = min(tm, 256)
    res = (x, (tm_down, tn), lambda jj, i: (i, jj))
    (x,) = _matmul(hid, w['w_down_dense'], j, [0], d, tm_down, tn, _epi_residual, [res], [F32], "ffn_down")
    return x, k[0], v[0]


def _deltanet_layer(x, tm, layer, j, w, batch, t, conv_buf, s0, decode):
    d = x.shape[1]
    act = F32 if decode else BF16
    h = _rmsnorm([x], w['norm_mix'][layer], act, tm)
    n_main = GD_CONV_CH + GD_VAL_WIDTH
    (proj,) = _matmul(h, w['w_in_gd'], j, [0], n_main, tm, 512, _epi_plain(1), [], [F32], "gd_in")
    (ba,) = _matmul(h, w['w_in_gd'], j, [n_main], LANES, tm, LANES, _epi_plain(1), [], [F32], "gd_in_ba")
    bg = _gdn_gates(ba, w['a_log_gd'][j], w['dt_bias_gd'][j], tm)
    gnorm = w['gnorm_gd'][j].reshape(1, GD_HEAD_DIM)
    if decode:
        conv_in = jnp.pad(proj.reshape(batch, 1, -1), ((0, 0), (0, SUBLANES - 1), (0, 0)))
        qkv = _gdn_conv(conv_in.reshape(batch * SUBLANES, -1), conv_buf, w['conv_w_gd'], j, batch,
                        SUBLANES, GD_K_HEADS)
        qkv = qkv.reshape(batch, SUBLANES, -1)[:, :1]
        z = proj[:, GD_CONV_CH:n_main].reshape(batch, 1, GD_VAL_WIDTH)
        o, s_fin = _gdn_step(qkv, z, bg.reshape(batch, 1, LANES), gnorm, s0)
        o = o.reshape(batch, GD_VAL_WIDTH)
    else:
        qkv = _gdn_conv(proj, conv_buf, w['conv_w_gd'], j, batch, t, 4)
        g_t = bg[:, GD_V_HEADS:2 * GD_V_HEADS].reshape(batch, t // CHUNK, CHUNK, GD_K_HEADS, 2)
        g_t = g_t.transpose(0, 1, 3, 4, 2).reshape(batch, t // CHUNK, GD_K_HEADS, 2 * CHUNK)
        o, s_fin = _gdn_chunks(qkv, proj, bg, g_t, gnorm, s0, batch, t)
    res = (x, (tm, 512), lambda jj, i: (i, jj))
    (x,) = _matmul(o, w['w_o_gd'], j, [0], d, tm, 512, _epi_residual, [res], [F32], "gd_out")
    h = _rmsnorm([x], w['norm_ffn'][layer], act, tm)
    gates = _router(h, w['w_router'], j, tm)
    w_gu = w['w_gu_moe'][j]
    w_down = w['w_down_moe'][j]
    d_ffe = w_down.shape[1]
    if not decode:
        return x, _moe_routed(h, gates, w_gu, w_down), s_fin, proj
    h = h.astype(BF16)
    gate_x = (gates, (tm, N_EXPERTS), lambda jj, i: (i, 0))
    for e in range(N_EXPERTS):
        (hid,) = _matmul(h, w_gu, e, [0, d_ffe], d_ffe, tm, 256, _epi_swiglu, [], [BF16], "moe_gu")
        res = (x, (tm, 512), lambda jj, i: (i, jj))
        (x,) = _matmul(hid, w_down, e, [0], d, tm, 512, _epi_gated_residual(e), [res, gate_x], [F32],
                       "moe_down")
    return x, None, s_fin, proj


def kernel(x_prompt, x_sample, cache_k, cache_v, state_delta, state_conv, page_table, norm_mix, norm_ffn, norm_final, w_qkv_da, lambda_q1, lambda_k1, lambda_q2, lambda_k2, subln_da, w_o_da, w_in_gd, conv_w_gd, a_log_gd, dt_bias_gd, gnorm_gd, w_o_gd, w_gu_dense, w_down_dense, w_router, w_gu_moe, w_down_moe):
    w = dict(norm_mix=norm_mix, norm_ffn=norm_ffn, w_qkv_da=w_qkv_da, lambda_q1=lambda_q1,
             lambda_k1=lambda_k1, lambda_q2=lambda_q2, lambda_k2=lambda_k2, subln_da=subln_da,
             w_o_da=w_o_da, w_in_gd=w_in_gd, conv_w_gd=conv_w_gd, a_log_gd=a_log_gd,
             dt_bias_gd=dt_bias_gd, gnorm_gd=gnorm_gd, w_o_gd=w_o_gd, w_gu_dense=w_gu_dense,
             w_down_dense=w_down_dense, w_router=w_router, w_gu_moe=w_gu_moe, w_down_moe=w_down_moe)
    b, s, d = x_prompt.shape
    bd, t_dec, _ = x_sample.shape
    assert t_dec == 1
    n_pages = page_table.shape[1]
    past = n_pages * PAGE_SIZE
    tm_p = 512
    tm_s = bd * t_dec
    depth = norm_mix.shape[0]

    xp = x_prompt.reshape(b * s, d)
    xs = x_sample.reshape(bd * t_dec, d)
    cos_p, sin_p = _rope_tables(s, 0, s)
    cos_s, sin_s = _rope_tables(tm_s, past, t_dec)
    rope_p = (cos_p, sin_p, s // tm_p)
    rope_s = (cos_s, sin_s, 1)

    kp_rows, vp_rows, ks_rows, vs_rows = [], [], [], []
    sp_fin, cp_fin, ss_fin, cs_fin = [], [], [], []
    xp_terms = None
    for i in range(depth):
        j = i // 2
        if i % 2 == 0:
            def attend_prompt(q, k, v, lam_vecs, subln, lam_init):
                return _attn_prompt(q, k, v, lam_vecs, subln, b, s, lam_init)

            def attend_sample(q, k, v, lam_vecs, subln, lam_init, j=j):
                o = _attn_decode(q.reshape(bd, DA_SUB_HEADS, DA_HEAD_DIM), cache_k, cache_v, j, page_table,
                                 k.reshape(bd, DA_SUB_HEADS, DA_HEAD_DIM),
                                 v.reshape(bd, DA_HEADS, DA_V_DIM), lam_vecs, subln, lam_init)
                return o.reshape(bd, DA_HEADS * DA_V_DIM)

            xp, kp, vp = _attention_layer(xp, tm_p, i, j, w, rope_p, attend_prompt, False)
            xs, ks, vs = _attention_layer(xs, tm_s, i, j, w, rope_s, attend_sample, True)
            kp_rows.append(kp.reshape(b, s, DA_SUB_HEADS, DA_HEAD_DIM))
            vp_rows.append(vp.reshape(b, s, DA_HEADS, DA_V_DIM))
            ks_rows.append(ks.reshape(bd, t_dec, DA_SUB_HEADS, DA_HEAD_DIM))
            vs_rows.append(vs.reshape(bd, t_dec, DA_HEADS, DA_V_DIM))
        else:
            zero_buf = jnp.zeros((b * SUBLANES, GD_CONV_CH), F32)
            zero_state = jnp.zeros((b, GD_V_HEADS, GD_HEAD_DIM, GD_HEAD_DIM), F32)
            assert i == depth - 1
            xp, moe_p, sp, proj_p = _deltanet_layer(xp, tm_p, i, j, w, b, s, zero_buf, zero_state, False)
            xp_terms = [xp, moe_p]
            buf_s = jnp.pad(state_conv[j], ((0, 0), (SUBLANES - (CONV_WIDTH - 1), 0), (0, 0)))
            xs, _, ss, proj_s = _deltanet_layer(xs, tm_s, i, j, w, bd, t_dec,
                                                buf_s.reshape(bd * SUBLANES, GD_CONV_CH), state_delta[j], True)
            sp_fin.append(sp)
            ss_fin.append(ss)
            cp_fin.append(proj_p.reshape(b, s, -1)[:, s - (CONV_WIDTH - 1):, :GD_CONV_CH])
            cs_all = jnp.concatenate([state_conv[j], proj_s.reshape(bd, t_dec, -1)[:, :, :GD_CONV_CH]], axis=1)
            cs_fin.append(cs_all[:, t_dec:])
    y_prompt = _rmsnorm(xp_terms or [xp], norm_final, F32, tm_p).reshape(b, s, d)
    y_sample = _rmsnorm([xs], norm_final, F32, tm_s).reshape(bd, t_dec, d)
    return (y_prompt, y_sample,
            jnp.stack(kp_rows), jnp.stack(vp_rows), jnp.stack(ks_rows), jnp.stack(vs_rows),
            jnp.stack(sp_fin), jnp.stack(cp_fin), jnp.stack(ss_fin), jnp.stack(cs_fin))
```
